```python
import jax, jax.numpy as jnp
from jax import lax
import numpy as np

D_MODEL = 1024
BATCH = 16
SEQ = 2048
DEPTH = 4

HEAD_DIM = 64
HEADS_PER_GROUP = D_MODEL // 256
DILATED_GROUPS = ((128, 1), (512, 4), (2048, 16))
N_ATTN_GROUPS = len(DILATED_GROUPS)
ATTN_WIDTH = N_ATTN_GROUPS * HEADS_PER_GROUP * HEAD_DIM
ATTN_OUT_WIDTH = HEADS_PER_GROUP * HEAD_DIM
ROPE_DIM = HEAD_DIM // 4
ROPE_THETA = 500000.0
MASK_VALUE = -1e30
POOL_WINDOWS = (2, 4, 8, 16)
N_POOL_GROUPS = len(POOL_WINDOWS)
POOL_WIDTH = D_MODEL // 2
POOL_GROUP_DIM = POOL_WIDTH // N_POOL_GROUPS
IN_COLS = 3 * ATTN_WIDTH + POOL_WIDTH + 2 * D_MODEL
N_EXPERTS = 32
TOP_K = 4
D_FF_EXPERT = D_MODEL
SWIGLU_ALPHA = 1.702
SWIGLU_LIMIT = 7.0
MOE_BLOCK = 128
NORM_EPS = 1e-5

kernel_name = "hybrid_dilated_pool_moe_encoder"


def rmsnorm(x, g):
    xf = x.astype(jnp.float32)
    y = xf * lax.rsqrt(jnp.mean(xf * xf, axis=-1, keepdims=True) + NORM_EPS)
    return (y * g.astype(jnp.float32)).astype(x.dtype)


def rope_tables(S):
    inv = ROPE_THETA ** (-jnp.arange(0, ROPE_DIM, 2, dtype=jnp.float32) / ROPE_DIM)
    ang = jnp.arange(S, dtype=jnp.float32)[:, None] * inv[None, :]
    return jnp.cos(ang), jnp.sin(ang)


def apply_partial_rope(t, cos, sin):
    half = ROPE_DIM // 2
    tf = t[..., :ROPE_DIM].astype(jnp.float32)
    t1, t2 = tf[..., :half], tf[..., half:]
    c = cos[None, :, None, None, :]
    s = sin[None, :, None, None, :]
    rot = jnp.concatenate([t1 * c - t2 * s, t2 * c + t1 * s], axis=-1).astype(t.dtype)
    return jnp.concatenate([rot, t[..., ROPE_DIM:]], axis=-1)


def dilated_window_attention(q, k, v, window, dilation):
    B, S, H, Dh = q.shape
    n_side = window // (2 * dilation)
    blk = n_side
    L = S // dilation
    nb = -(-L // blk)
    Lp = nb * blk

    def residue(t):
        return t.reshape(B, L, dilation, H, Dh).transpose(0, 2, 1, 3, 4)

    qr, kr, vr = residue(q), residue(k), residue(v)
    qb = jnp.pad(qr, ((0, 0), (0, 0), (0, Lp - L), (0, 0), (0, 0))).reshape(B, dilation, nb, blk, H, Dh)

    def key_windows(t):
        tp = jnp.pad(t, ((0, 0), (0, 0), (blk, blk + Lp - L), (0, 0), (0, 0)))
        tp = tp.reshape(B, dilation, nb + 2, blk, H, Dh)
        return jnp.concatenate([tp[:, :, :-2], tp[:, :, 1:-1], tp[:, :, 2:]], axis=3)

    kw, vw = key_windows(kr), key_windows(vr)
    scores = jnp.einsum('brnqhd,brnkhd->brnhqk', qb, kw,
                        preferred_element_type=jnp.float32) * (Dh ** -0.5)
    qpos = jnp.arange(nb)[:, None] * blk + jnp.arange(blk)[None, :]
    kpos = jnp.arange(nb)[:, None] * blk - blk + jnp.arange(3 * blk)[None, :]
    rel = kpos[:, None, :] - qpos[:, :, None]
    valid = (jnp.abs(rel) <= n_side) & (kpos[:, None, :] >= 0) & (kpos[:, None, :] < L)
    scores = jnp.where(valid[None, None, :, None], scores, MASK_VALUE)
    lse = jax.nn.logsumexp(scores, axis=-1)
    probs = jnp.exp(scores - lse[..., None])
    out = jnp.einsum('brnhqk,brnkhd->brnqhd', probs.astype(v.dtype), vw)
    out = out.reshape(B, dilation, Lp, H, Dh)[:, :, :L].transpose(0, 2, 1, 3, 4).reshape(B, S, H, Dh)
    lse = lse.transpose(0, 1, 2, 4, 3).reshape(B, dilation, Lp, H)[:, :, :L]
    lse = lse.transpose(0, 2, 1, 3).reshape(B, S, H)
    return out, lse


def multiscale_pool(u, pool_w, pool_scale):
    B, S, _ = u.shape
    uf = u.astype(jnp.float32).reshape(B, S, N_POOL_GROUPS, POOL_GROUP_DIM)
    csum = jnp.concatenate([jnp.zeros_like(uf[:, :1]), jnp.cumsum(uf, axis=1)], axis=1)
    half = jnp.array(POOL_WINDOWS, dtype=jnp.int32) // 2
    pos = jnp.arange(S, dtype=jnp.int32)[:, None]
    lo = jnp.clip(pos - half[None, :], 0, S)
    hi = jnp.clip(pos + half[None, :], 0, S)
    gidx = jnp.arange(N_POOL_GROUPS)[None, :]
    window_sum = csum[:, hi, gidx] - csum[:, lo, gidx]
    mean = window_sum / (hi - lo).astype(jnp.float32)[None, :, :, None]
    y = jnp.einsum('bsgc,gcd->bsgd', mean - uf, pool_w.astype(jnp.float32))
    return (y.reshape(B, S, POOL_WIDTH) * pool_scale.astype(jnp.float32)).astype(u.dtype)


def hybrid_mixer(h, w_in, pool_w, pool_scale, w_attn_branch, w_pool_branch, w_out, cos, sin):
    B, S, _ = h.shape
    proj = h @ w_in
    splits = np.cumsum([ATTN_WIDTH, ATTN_WIDTH, ATTN_WIDTH, POOL_WIDTH, D_MODEL]).tolist()
    q, k, v, u, gate_a, gate_p = jnp.split(proj, splits, axis=-1)

    def heads(t):
        return t.reshape(B, S, N_ATTN_GROUPS, HEADS_PER_GROUP, HEAD_DIM)

    q = apply_partial_rope(heads(q), cos, sin)
    k = apply_partial_rope(heads(k), cos, sin)
    v = heads(v)
    outs, lses = [], []
    for g, (window, dilation) in enumerate(DILATED_GROUPS):
        o_g, lse_g = dilated_window_attention(q[:, :, g], k[:, :, g], v[:, :, g], window, dilation)
        outs.append(o_g)
        lses.append(lse_g)
    alpha = jax.nn.softmax(jnp.stack(lses, axis=0), axis=0)
    attn = jnp.einsum('gbsh,gbshd->bshd', alpha, jnp.stack(outs, axis=0).astype(jnp.float32))
    attn = attn.reshape(B, S, ATTN_OUT_WIDTH).astype(h.dtype)
    pool = multiscale_pool(u, pool_w, pool_scale)
    merged = (jax.nn.sigmoid(gate_a) * (attn @ w_attn_branch)
              + jax.nn.sigmoid(gate_p) * (pool @ w_pool_branch))
    return merged @ w_out


def clamped_swiglu(z):
    glu, lin = z[..., :D_FF_EXPERT], z[..., D_FF_EXPERT:]
    glu = jnp.minimum(glu, SWIGLU_LIMIT)
    lin = jnp.clip(lin, -SWIGLU_LIMIT, SWIGLU_LIMIT)
    return glu * jax.nn.sigmoid(SWIGLU_ALPHA * glu) * (lin + 1)


def moe_ffn(h, router_w, router_b, w_gu, b_gu, w_down, b_down):
    B, S, D = h.shape
    T = B * S
    tokens = h.reshape(T, D)
    logits = (tokens @ router_w).astype(jnp.float32) + router_b.astype(jnp.float32)
    top_logits, top_idx = lax.top_k(logits, TOP_K)
    gates = jax.nn.softmax(top_logits, axis=-1)
    n_assign = T * TOP_K
    flat_e = top_idx.reshape(n_assign)
    order = jnp.argsort(flat_e)
    sorted_e = flat_e[order]
    counts = jnp.bincount(flat_e, length=N_EXPERTS)
    padded = (counts + MOE_BLOCK - 1) // MOE_BLOCK * MOE_BLOCK
    padded_end = jnp.cumsum(padded)
    start = jnp.cumsum(counts) - counts
    dest = (padded_end - padded)[sorted_e] + jnp.arange(n_assign) - start[sorted_e]
    n_blocks = -(-n_assign // MOE_BLOCK) + N_EXPERTS
    n_rows = n_blocks * MOE_BLOCK
    row_token = jnp.full((n_rows,), T, dtype=jnp.int32).at[dest].set((order // TOP_K).astype(jnp.int32))
    row_gate = jnp.zeros((n_rows,), jnp.float32).at[dest].set(gates.reshape(n_assign)[order])
    block_expert = jnp.minimum(
        jnp.searchsorted(padded_end, jnp.arange(n_blocks) * MOE_BLOCK, side='right'), N_EXPERTS - 1)
    tokens_pad = jnp.concatenate([tokens, jnp.zeros((1, D), tokens.dtype)], axis=0)
    x_rows = tokens_pad[row_token].reshape(n_blocks, MOE_BLOCK, D)

    def expert_block(args):
        xb, e = args
        z = xb @ w_gu[e] + b_gu[e]
        return clamped_swiglu(z) @ w_down[e] + b_down[e]

    y = lax.map(expert_block, (x_rows, block_expert)).reshape(n_rows, D)
    out = jnp.zeros((T + 1, D), jnp.float32).at[row_token].add(y.astype(jnp.float32) * row_gate[:, None])
    return out[:T].astype(h.dtype).reshape(B, S, D)


def setup_inputs(seed: int = 0) -> dict:
    key = jax.random.key(seed)
    ks = jax.random.split(key, 20)

    def nrm(k, shape, scale):
        return jax.random.normal(k, shape, jnp.float32) * scale

    return {
        "x": nrm(ks[0], (BATCH, SEQ, D_MODEL), 1.0),
        "c": nrm(ks[1], (BATCH, D_MODEL), 1.0),
        "ada_w": nrm(ks[2], (DEPTH, D_MODEL, 6 * D_MODEL), 0.5 * D_MODEL ** -0.5),
        "ada_b": nrm(ks[3], (DEPTH, 6 * D_MODEL), 0.01),
        "norm1_g": 1.0 + nrm(ks[4], (DEPTH, D_MODEL), 0.05),
        "w_in": nrm(ks[5], (DEPTH, D_MODEL, IN_COLS), D_MODEL ** -0.5),
        "pool_w": nrm(ks[6], (DEPTH, N_POOL_GROUPS, POOL_GROUP_DIM, POOL_GROUP_DIM), POOL_GROUP_DIM ** -0.5),
        "pool_scale": 1.0 + nrm(ks[7], (DEPTH, POOL_WIDTH), 0.05),
        "w_attn_branch": nrm(ks[8], (DEPTH, ATTN_OUT_WIDTH, D_MODEL), ATTN_OUT_WIDTH ** -0.5),
        "w_pool_branch": nrm(ks[9], (DEPTH, POOL_WIDTH, D_MODEL), POOL_WIDTH ** -0.5),
        "w_out": nrm(ks[10], (DEPTH, D_MODEL, D_MODEL), D_MODEL ** -0.5),
        "norm2_g": 1.0 + nrm(ks[11], (DEPTH, D_MODEL), 0.05),
        "router_w": nrm(ks[12], (DEPTH, D_MODEL, N_EXPERTS), D_MODEL ** -0.5),
        "router_b": nrm(ks[13], (DEPTH, N_EXPERTS), 0.01),
        "w_gu": nrm(ks[14], (DEPTH, N_EXPERTS, D_MODEL, 2 * D_FF_EXPERT), D_MODEL ** -0.5),
        "b_gu": nrm(ks[15], (DEPTH, N_EXPERTS, 2 * D_FF_EXPERT), 0.01),
        "w_down": nrm(ks[16], (DEPTH, N_EXPERTS, D_FF_EXPERT, D_MODEL), D_FF_EXPERT ** -0.5),
        "b_down": nrm(ks[17], (DEPTH, N_EXPERTS, D_MODEL), 0.01),
        "final_g": 1.0 + nrm(ks[18], (D_MODEL,), 0.05),
    }


def reference(x, c, ada_w, ada_b, norm1_g, w_in, pool_w, pool_scale, w_attn_branch,
              w_pool_branch, w_out, norm2_g, router_w, router_b, w_gu, b_gu, w_down,
              b_down, final_g):
    S = x.shape[1]
    cos, sin = rope_tables(S)
    c_act = jax.nn.silu(c)
    for l in range(DEPTH):
        mod = c_act @ ada_w[l] + ada_b[l]
        sh1, sc1, g1, sh2, sc2, g2 = [m[:, None, :] for m in jnp.split(mod, 6, axis=-1)]
        h = rmsnorm(x, norm1_g[l]) * (1 + sc1) + sh1
        x = x + g1 * hybrid_mixer(h, w_in[l], pool_w[l], pool_scale[l], w_attn_branch[l],
                                  w_pool_branch[l], w_out[l], cos, sin)
        h = rmsnorm(x, norm2_g[l]) * (1 + sc2) + sh2
        x = x + g2 * moe_ffn(h, router_w[l], router_b[l], w_gu[l], b_gu[l], w_down[l], b_down[l])
    return rmsnorm(x, final_g)
```

```python
import functools

import numpy as np
import jax
import jax.numpy as jnp
from jax import lax
from jax.experimental import pallas as pl
from jax.experimental.pallas import tpu as pltpu

F32 = jnp.float32
BF16 = jnp.bfloat16
I32 = jnp.int32
U32 = jnp.uint32

HEAD_DIM = 64
HEADS_PER_GROUP = 4
DILATIONS = (1, 4, 16)
N_SIDE = 64
N_GROUPS = len(DILATIONS)
GROUP_WIDTH = HEADS_PER_GROUP * HEAD_DIM
ATTN_WIDTH = N_GROUPS * GROUP_WIDTH
ROPE_DIM = HEAD_DIM // 4
ROPE_HALF = ROPE_DIM // 2
ROPE_THETA = 500000.0
MASK_VALUE = -1e30
POOL_WINDOWS = (2, 4, 8, 16)
POOL_WIDTH = 512
N_EXPERTS = 32
TOP_K = 4
SWIGLU_ALPHA = 1.702
SWIGLU_LIMIT = 7.0
NORM_EPS = 1e-5

LANES = 128
SUBLANES = 8
VMEM_LIMIT = 56 * 1024 * 1024

TM = 512
CHUNK = SUBLANES
RB = 256
BQ = 128
POOL_TILE = 256
POOL_HALO = 16


def _round_up(a, m):
    return (a + m - 1) // m * m


R_T = _round_up(TOP_K * TM + N_EXPERTS * (CHUNK - 1), 256)
CPT = R_T // CHUNK
CPB = RB // CHUNK
MAX_USED_CHUNKS = (TOP_K * TM + N_EXPERTS * (CHUNK - 1)) // CHUNK
assert MAX_USED_CHUNKS < CPT
ZERO_CHUNK = CPT - 1


def _cparams(n_axes):
    return pltpu.CompilerParams(
        dimension_semantics=("arbitrary",) * n_axes, vmem_limit_bytes=VMEM_LIMIT)


def _norm_mod(x, g, scale, shift):
    ms = jnp.mean(x * x, axis=-1, keepdims=True)
    y = x * lax.rsqrt(ms + NORM_EPS) * g
    return y * (1.0 + scale) + shift


def _pack_bf16_pairs(v):
    w = v.shape[1] // 2
    bits = pltpu.bitcast(v, U32)
    return (bits[:, :w] >> 16) | (bits[:, w:] & jnp.uint32(0xFFFF0000))


def _unpack_bf16_pairs(p):
    lo = pltpu.bitcast(p << 16, F32).astype(BF16)
    hi = pltpu.bitcast(p & jnp.uint32(0xFFFF0000), F32).astype(BF16)
    return jnp.concatenate([lo, hi], axis=1)


def _ada_kernel(c_ref, w_ref, b_ref, o_ref):
    c = c_ref[...]
    ca = c * jax.nn.sigmoid(c)
    o_ref[0] = jnp.dot(ca, w_ref[0], preferred_element_type=F32,
                       precision=lax.Precision.HIGHEST) + b_ref[0]


def _ada(c, ada_w, ada_b):
    depth, d, n = ada_w.shape
    b = c.shape[0]
    tn = 1536
    return pl.pallas_call(
        _ada_kernel,
        grid=(depth, n // tn),
        in_specs=[
            pl.BlockSpec((b, d), lambda l, j: (0, 0)),
            pl.BlockSpec((1, d, tn), lambda l, j: (l, 0, j)),
            pl.BlockSpec((1, 1, tn), lambda l, j: (l, 0, j)),
        ],
        out_specs=pl.BlockSpec((1, b, tn), lambda l, j: (l, 0, j)),
        out_shape=jax.ShapeDtypeStruct((depth, b, n), F32),
        compiler_params=_cparams(2),
        name="ada",
    )(c, ada_w, ada_b.reshape(depth, 1, n))


def _inproj_kernel(x_ref, mod_ref, g_ref, w_ref, rc_ref, rsa_ref, rsb_ref,
                   q0_ref, q1_ref, q2_ref, u_ref):
    h = _norm_mod(x_ref[...], g_ref[...], mod_ref[1:2, :], mod_ref[0:1, :])
    hb = h.astype(BF16)
    rc, rsa, rsb = rc_ref[...], rsa_ref[...], rsb_ref[...]
    for g, out_ref in enumerate((q0_ref, q1_ref, q2_ref)):
        p = jnp.dot(hb, w_ref[:, g * ATTN_WIDTH:(g + 1) * ATTN_WIDTH],
                    preferred_element_type=F32)
        for j in range(6):
            s = p[:, j * LANES:(j + 1) * LANES]
            if j < 4:
                s = (s * rc + pltpu.roll(s, LANES - ROPE_HALF, 1) * rsa
                     + pltpu.roll(s, ROPE_HALF, 1) * rsb)
            if j < 2:
                s = s * (HEAD_DIM ** -0.5)
            out_ref[j] = s.astype(BF16)
    u_ref[...] = jnp.dot(hb, w_ref[:, N_GROUPS * ATTN_WIDTH:],
                         preferred_element_type=F32)


def _inproj(xt, mod_l, g1n, w_qkvu, rope, batch, seq):
    t, d = xt.shape
    tpb = seq // TM
    rc, rsa, rsb = rope
    row = lambda i: (i, 0)
    const = lambda i: (0, 0)
    rope_spec = pl.BlockSpec((TM, LANES), lambda i: (i % tpb, 0))
    slab_spec = pl.BlockSpec((None, 6, TM, LANES), lambda i: (i // tpb, 0, i % tpb, 0))
    slab_shape = jax.ShapeDtypeStruct((batch, 6, seq, LANES), BF16)
    return pl.pallas_call(
        _inproj_kernel,
        grid=(t // TM,),
        in_specs=[
            pl.BlockSpec((TM, d), row),
            pl.BlockSpec((None, 6, d), lambda i: (i // tpb, 0, 0)),
            pl.BlockSpec((1, d), const),
            pl.BlockSpec(w_qkvu.shape, const),
            rope_spec, rope_spec, rope_spec,
        ],
        out_specs=[slab_spec, slab_spec, slab_spec, pl.BlockSpec((TM, POOL_WIDTH), row)],
        out_shape=[slab_shape, slab_shape, slab_shape,
                   jax.ShapeDtypeStruct((t, POOL_WIDTH), F32)],
        compiler_params=_cparams(1),
        name="inproj",
    )(xt, mod_l, g1n, w_qkvu, rc, rsa, rsb)


def _attn_kernel(qkv_ref, o_ref, lse_ref, *, length, dil):
    bq = min(BQ, length)
    kw = min(length, bq + 2 * N_SIDE)
    nblk = length // bq
    is_a = lax.broadcasted_iota(I32, (1, LANES), 1) < HEAD_DIM
    rel0 = (lax.broadcasted_iota(I32, (bq, kw), 1)
            - lax.broadcasted_iota(I32, (bq, kw), 0))
    nt = (((1,), (1,)), ((), ()))

    for r in range(dil):
        cs = slice(r * LANES, (r + 1) * LANES)
        for p in range(2):
            def block(i, carry, cs=cs, p=p):
                m0 = pl.multiple_of(i * bq, bq)
                start = pl.multiple_of(jnp.clip(m0 - N_SIDE, 0, length - kw), N_SIDE)
                q2 = qkv_ref[p, pl.ds(m0, bq), cs]
                k2 = qkv_ref[2 + p, pl.ds(start, kw), cs]
                v2 = qkv_ref[4 + p, pl.ds(start, kw), cs]
                valid = jnp.abs(rel0 + (start - m0)) <= N_SIDE
                zero = jnp.zeros_like(q2)
                outs, lses = [], []
                for qh in (jnp.where(is_a, q2, zero), jnp.where(is_a, zero, q2)):
                    s = lax.dot_general(qh, k2, nt, preferred_element_type=F32)
                    s = jnp.where(valid, s, MASK_VALUE)
                    mx = jnp.max(s, axis=-1, keepdims=True)
                    e = jnp.exp(s - mx)
                    den = jnp.sum(e, axis=-1, keepdims=True)
                    o = jnp.dot(e.astype(BF16), v2, preferred_element_type=F32)
                    outs.append(o / den)
                    lses.append(mx + jnp.log(den))
                o_ref[p, pl.ds(m0, bq), cs] = jnp.where(is_a, outs[0], outs[1])
                lse_ref[p, pl.ds(m0, bq), cs] = jnp.where(is_a, lses[0], lses[1])
                return carry
            lax.fori_loop(0, nblk, block, 0)


def _attention(qkv, dil):
    batch, _, seq, _ = qkv.shape
    length = seq // dil
    width = dil * LANES
    qv = qkv.reshape(batch, 6, length, width)
    out_shape = jax.ShapeDtypeStruct((batch, 2, length, width), F32)
    out_spec = pl.BlockSpec((None, 2, length, width), lambda b: (b, 0, 0, 0))
    o, lse = pl.pallas_call(
        functools.partial(_attn_kernel, length=length, dil=dil),
        grid=(batch,),
        in_specs=[pl.BlockSpec((None, 6, length, width), lambda b: (b, 0, 0, 0))],
        out_specs=[out_spec, out_spec],
        out_shape=[out_shape, out_shape],
        compiler_params=_cparams(1),
        name=f"attn_d{dil}",
    )(qv)
    return o.reshape(batch, 2, seq, LANES), lse.reshape(batch, 2, seq, LANES)


def _pool_kernel(u_ref, pw_ref, ps_ref, o_ref, *, seq):
    win = POOL_TILE + 2 * POOL_HALO
    rel0 = (lax.broadcasted_iota(I32, (POOL_TILE, win), 1)
            - lax.broadcasted_iota(I32, (POOL_TILE, win), 0))
    row = lax.broadcasted_iota(I32, (POOL_TILE, 1), 0)

    def tile(i, carry):
        t0 = pl.multiple_of(i * POOL_TILE, POOL_TILE)
        start = pl.multiple_of(jnp.clip(t0 - POOL_HALO, 0, seq - win), SUBLANES)
        rel = rel0 + (start - t0)
        pos = row + t0
        for g, w in enumerate(POOL_WINDOWS):
            half = w // 2
            cs = slice(g * LANES, (g + 1) * LANES)
            band = jnp.where(jnp.abs(2 * rel + 1) < w, 1.0, 0.0).astype(BF16)
            uw = u_ref[pl.ds(start, win), cs]
            hi = uw.astype(BF16)
            lo = (uw - hi.astype(F32)).astype(BF16)
            wsum = (jnp.dot(band, hi, preferred_element_type=F32)
                    + jnp.dot(band, lo, preferred_element_type=F32))
            cnt = (jnp.minimum(pos + half, seq) - jnp.maximum(pos - half, 0)).astype(F32)
            diff = wsum / cnt - u_ref[pl.ds(t0, POOL_TILE), cs]
            y = jnp.dot(diff.astype(BF16), pw_ref[g], preferred_element_type=F32)
            o_ref[pl.ds(t0, POOL_TILE), cs] = (y * ps_ref[:, cs]).astype(BF16)
        return carry

    lax.fori_loop(0, seq // POOL_TILE, tile, 0)


def _pool(u, pool_w, pool_scale, batch, seq):
    t = u.shape[0]
    return pl.pallas_call(
        functools.partial(_pool_kernel, seq=seq),
        grid=(batch,),
        in_specs=[
            pl.BlockSpec((seq, POOL_WIDTH), lambda b: (b, 0)),
            pl.BlockSpec(pool_w.shape, lambda b: (0, 0, 0)),
            pl.BlockSpec((1, POOL_WIDTH), lambda b: (0, 0)),
        ],
        out_specs=pl.BlockSpec((seq, POOL_WIDTH), lambda b: (b, 0)),
        out_shape=jax.ShapeDtypeStruct((t, POOL_WIDTH), BF16),
        compiler_params=_cparams(1),
        name="pool",
    )(u, pool_w, pool_scale)


def _epilogue_kernel(x_ref, mod_ref, g1n_ref, wg_ref,
                     o0_ref, l0_ref, o1_ref, l1_ref, o2_ref, l2_ref,
                     pool_ref, wa_ref, wp_ref, wo_ref, g2n_ref,
                     rwh_ref, rwl_ref, rb_ref, triu_ref,
                     xmid_ref, xs_ref, lrow_ref, gate_ref, cmeta_ref):
    d = x_ref.shape[1]
    x = x_ref[...]
    sh1, sc1, g1 = mod_ref[0:1, :], mod_ref[1:2, :], mod_ref[2:3, :]
    sh2, sc2 = mod_ref[3:4, :], mod_ref[4:5, :]

    hb = _norm_mod(x, g1n_ref[...], sc1, sh1).astype(BF16)
    gates = jax.nn.sigmoid(jnp.dot(hb, wg_ref[...], preferred_element_type=F32))
    halves = []
    for p in range(2):
        a0, a1, a2 = l0_ref[p], l1_ref[p], l2_ref[p]
        mx = jnp.maximum(jnp.maximum(a0, a1), a2)
        w0, w1, w2 = jnp.exp(a0 - mx), jnp.exp(a1 - mx), jnp.exp(a2 - mx)
        num = w0 * o0_ref[p] + w1 * o1_ref[p] + w2 * o2_ref[p]
        halves.append((num / (w0 + w1 + w2)).astype(BF16))
    attn = jnp.concatenate(halves, axis=1)
    a_br = jnp.dot(attn, wa_ref[...], preferred_element_type=F32)
    p_br = jnp.dot(pool_ref[...], wp_ref[...], preferred_element_type=F32)
    merged = gates[:, :d] * a_br + gates[:, d:] * p_br
    mix = jnp.dot(merged.astype(BF16), wo_ref[...], preferred_element_type=F32)
    xm = x + g1 * mix
    xmid_ref[...] = xm

    h2 = _norm_mod(xm, g2n_ref[...], sc2, sh2)
    h2b = h2.astype(BF16)
    h2l = (h2 - h2b.astype(F32)).astype(BF16)
    nt = (((1,), (1,)), ((), ()))
    rwh = rwh_ref[...]
    logits = (lax.dot_general(rwh, h2b, nt, preferred_element_type=F32)
              + lax.dot_general(rwl_ref[...], h2b, nt, preferred_element_type=F32)
              + lax.dot_general(rwh, h2l, nt, preferred_element_type=F32)
              + rb_ref[...])
    tm = logits.shape[1]

    e_io = lax.broadcasted_iota(I32, (N_EXPERTS, tm), 0)
    work = logits
    sels, vals = [], []
    for _ in range(TOP_K):
        mx = jnp.max(work, axis=0, keepdims=True)
        idx = jnp.min(jnp.where(work == mx, e_io, N_EXPERTS), axis=0, keepdims=True)
        sel = e_io == idx
        sels.append(sel)
        vals.append(mx)
        work = jnp.where(sel, -jnp.inf, work)
    exps = [jnp.exp(v - vals[0]) for v in vals]
    den = exps[0] + exps[1] + exps[2] + exps[3]
    for k in range(TOP_K):
        gate_ref[k:k + 1, :] = exps[k] / den

    triu = triu_ref[...]
    ohs = [jnp.where(s, 1.0, 0.0) for s in sels]
    withins = [jnp.dot(o.astype(BF16), triu, preferred_element_type=F32) for o in ohs]
    cnts = [jnp.sum(o, axis=1, keepdims=True) for o in ohs]
    n_e = cnts[0] + cnts[1] + cnts[2] + cnts[3]
    chunks = jnp.floor((n_e + (CHUNK - 1)) * (1.0 / CHUNK))
    chunks_l = jnp.broadcast_to(chunks, (N_EXPERTS, LANES))
    ltri = jnp.where(lax.broadcasted_iota(I32, (N_EXPERTS, N_EXPERTS), 0)
                     > lax.broadcasted_iota(I32, (N_EXPERTS, N_EXPERTS), 1), 1.0, 0.0)
    rstart_l = jnp.dot(ltri.astype(BF16), chunks_l.astype(BF16), preferred_element_type=F32)
    cmeta_ref[0] = chunks_l.astype(I32)
    cmeta_ref[1] = rstart_l.astype(I32)
    base = rstart_l[:, 0:1] * CHUNK
    lrows = []
    for k in range(TOP_K):
        lr = jnp.sum(ohs[k] * (base + withins[k]), axis=0, keepdims=True).astype(I32)
        lrow_ref[k:k + 1, :] = lr
        lrows.append(lr)
        base = base + cnts[k]

    blk = 256
    for r0 in range(0, R_T, blk):
        r_io = lax.broadcasted_iota(I32, (blk, tm), 0) + r0
        hit = jnp.where(r_io == lrows[0], 1.0,
              jnp.where(r_io == lrows[1], 1.0,
              jnp.where(r_io == lrows[2], 1.0,
              jnp.where(r_io == lrows[3], 1.0, 0.0))))
        rows = jnp.dot(hit.astype(BF16), h2b, preferred_element_type=F32)
        xs_ref[r0:r0 + blk, :] = _pack_bf16_pairs(rows)


def _epilogue(xt, mod_l, g1n, wg, attn_outs, pool, wa, wp, wo, g2n, rwh, rwl, rb, triu,
              batch, seq):
    t, d = xt.shape
    n_tiles = t // TM
    tpb = seq // TM
    row = lambda i: (i, 0)
    const = lambda i: (0, 0)
    half_spec = pl.BlockSpec((None, 2, TM, LANES), lambda i: (i // tpb, 0, i % tpb, 0))
    full = lambda a: pl.BlockSpec(a.shape, const)
    return pl.pallas_call(
        _epilogue_kernel,
        grid=(n_tiles,),
        in_specs=[
            pl.BlockSpec((TM, d), row),
            pl.BlockSpec((None, 6, d), lambda i: (i // tpb, 0, 0)),
            full(g1n), full(wg),
            half_spec, half_spec, half_spec, half_spec, half_spec, half_spec,
            pl.BlockSpec((TM, POOL_WIDTH), row),
            full(wa), full(wp), full(wo), full(g2n),
            full(rwh), full(rwl), full(rb), full(triu),
        ],
        out_specs=[
            pl.BlockSpec((TM, d), row),
            pl.BlockSpec((R_T, d // 2), row),
            pl.BlockSpec((TOP_K, TM), lambda i: (0, i)),
            pl.BlockSpec((TOP_K, TM), lambda i: (0, i)),
            pl.BlockSpec((None, 2, N_EXPERTS, LANES), lambda i: (i, 0, 0, 0)),
        ],
        out_shape=[
            jax.ShapeDtypeStruct((t, d), F32),
            jax.ShapeDtypeStruct((n_tiles * R_T, d // 2), U32),
            jax.ShapeDtypeStruct((TOP_K, t), I32),
            jax.ShapeDtypeStruct((TOP_K, t), F32),
            jax.ShapeDtypeStruct((n_tiles, 2, N_EXPERTS, LANES), I32),
        ],
        compiler_params=_cparams(1),
        name="epilogue",
    )(xt, mod_l, g1n, wg, *attn_outs, pool, wa, wp, wo, g2n, rwh, rwl, rb, triu)


def _route_kernel(cnt_ref, rs_ref, src_ref, bexp_ref, nact_ref, csrc_ref, *, n_tiles, n_blocks):
    def per_expert(e, seg):
        def per_tile(j, tot):
            n = cnt_ref[j, e]
            d0 = seg * CPB + tot
            s0 = j * CPT + rs_ref[j, e]

            def per_chunk(i, c):
                src_ref[d0 + i] = s0 + i
                csrc_ref[s0 + i] = d0 + i
                return c
            lax.fori_loop(0, n, per_chunk, 0)
            return tot + n
        tot = lax.fori_loop(0, n_tiles, per_tile, 0)
        nb = lax.shift_right_logical(tot + (CPB - 1), int(np.log2(CPB)))

        def pad_slot(s, c):
            src_ref[s] = ZERO_CHUNK
            return c
        lax.fori_loop(seg * CPB + tot, (seg + nb) * CPB, pad_slot, 0)

        def set_block(b, c):
            bexp_ref[seg + b] = e
            return c
        lax.fori_loop(0, nb, set_block, 0)
        return seg + nb

    nact = lax.fori_loop(0, N_EXPERTS, per_expert, 0)
    nact_ref[0] = nact

    def idle_block(b, c):
        bexp_ref[b] = N_EXPERTS - 1

        def pad_slot(s, c2):
            src_ref[b * CPB + s] = ZERO_CHUNK
            return c2
        lax.fori_loop(0, CPB, pad_slot, 0)
        return c
    lax.fori_loop(nact, n_blocks, idle_block, 0)

    def per_tile_tail(j, c):
        used = lax.fori_loop(0, N_EXPERTS, lambda e, u: u + cnt_ref[j, e], 0)

        def pad_slot(s, c2):
            csrc_ref[j * CPT + s] = 0
            return c2
        lax.fori_loop(used, CPT, pad_slot, 0)
        return c
    lax.fori_loop(0, n_tiles, per_tile_tail, 0)


def _route(cnt, rs, n_blocks):
    n_tiles = cnt.shape[0]
    smem = pl.BlockSpec(memory_space=pltpu.SMEM)
    return pl.pallas_call(
        functools.partial(_route_kernel, n_tiles=n_tiles, n_blocks=n_blocks),
        in_specs=[smem, smem],
        out_specs=[smem, smem, smem, smem],
        out_shape=[
            jax.ShapeDtypeStruct((n_blocks * CPB,), I32),
            jax.ShapeDtypeStruct((n_blocks,), I32),
            jax.ShapeDtypeStruct((1,), I32),
            jax.ShapeDtypeStruct((n_tiles * CPT,), I32),
        ],
        name="route",
    )(cnt, rs)


def _chunk_ring(table_ref, src_hbm, buf, sem, step, n_steps, n_chunks):
    slot = lax.rem(step, 2)

    def start(which, to_slot):
        def issue(i, c):
            pltpu.make_async_copy(src_hbm.at[table_ref[which * n_chunks + i]],
                                  buf.at[to_slot, i], sem.at[to_slot]).start()
            return c
        lax.fori_loop(0, n_chunks, issue, 0)

    @pl.when(step == 0)
    def _():
        start(0, 0)

    @pl.when(step + 1 < n_steps)
    def _():
        start(step + 1, 1 - slot)

    pltpu.make_async_copy(src_hbm.at[pl.ds(0, n_chunks)], buf.at[slot], sem.at[slot]).wait()
    return slot


def _expert_kernel(bexp_ref, nact_ref, src_ref, xs_hbm, wgu_ref, bgu_ref, wd_ref, bd_ref,
                   ys_ref, xbuf, sem, *, n_blocks):
    b = pl.program_id(0)
    slot = _chunk_ring(src_ref, xs_hbm, xbuf, sem, b, n_blocks, CPB)
    f = wd_ref.shape[0]

    @pl.when(b < nact_ref[0])
    def _():
        x = _unpack_bf16_pairs(xbuf[slot].reshape(RB, xbuf.shape[-1]))
        z = jnp.dot(x, wgu_ref[...], preferred_element_type=F32) + bgu_ref[...]
        glu = jnp.minimum(z[:, :f], SWIGLU_LIMIT)
        lin = jnp.clip(z[:, f:], -SWIGLU_LIMIT, SWIGLU_LIMIT)
        act = glu * jax.nn.sigmoid(SWIGLU_ALPHA * glu) * (lin + 1.0)
        y = jnp.dot(act.astype(BF16), wd_ref[...], preferred_element_type=F32) + bd_ref[...]
        ys_ref[...] = _pack_bf16_pairs(y.astype(BF16).astype(F32))

    @pl.when(b >= nact_ref[0])
    def _():
        ys_ref[...] = jnp.zeros_like(ys_ref)


def _experts(xs, src, bexp, nact, wgu, bgu, wd, bd, n_blocks):
    half = xs.shape[1]
    n_exp, d, f2 = wgu.shape
    f = wd.shape[1]
    xs3 = xs.reshape(xs.shape[0] // CHUNK, CHUNK, half)
    grid_spec = pltpu.PrefetchScalarGridSpec(
        num_scalar_prefetch=3,
        grid=(n_blocks,),
        in_specs=[
            pl.BlockSpec(memory_space=pl.ANY),
            pl.BlockSpec((None, d, f2), lambda b, be, na, sr: (be[b], 0, 0)),
            pl.BlockSpec((None, 1, f2), lambda b, be, na, sr: (be[b], 0, 0)),
            pl.BlockSpec((None, f, d), lambda b, be, na, sr: (be[b], 0, 0)),
            pl.BlockSpec((None, 1, d), lambda b, be, na, sr: (be[b], 0, 0)),
        ],
        out_specs=pl.BlockSpec((RB, half), lambda b, be, na, sr: (b, 0)),
        scratch_shapes=[
            pltpu.VMEM((2, CPB, CHUNK, half), U32),
            pltpu.SemaphoreType.DMA((2,)),
        ],
    )
    return pl.pallas_call(
        functools.partial(_expert_kernel, n_blocks=n_blocks),
        grid_spec=grid_spec,
        out_shape=jax.ShapeDtypeStruct((n_blocks * RB, half), U32),
        compiler_params=_cparams(1),
        name="experts",
    )(bexp, nact, src, xs3, wgu, bgu.reshape(n_exp, 1, f2), wd, bd.reshape(n_exp, 1, d))


def _combine_kernel(csrc_ref, xmid_ref, mod_ref, lrow_ref, gate_ref, fg_ref, ys_hbm,
                    out_ref, ybuf, sem, *, n_tiles, final):
    j = pl.program_id(0)
    slot = _chunk_ring(csrc_ref, ys_hbm, ybuf, sem, j, n_tiles, CPT)
    tm = xmid_ref.shape[0]
    y = _unpack_bf16_pairs(ybuf[slot].reshape(R_T, ybuf.shape[-1]))
    r_io = lax.broadcasted_iota(I32, (R_T, tm), 0)
    gt = jnp.zeros((R_T, tm), F32)
    for k in reversed(range(TOP_K)):
        gt = jnp.where(r_io == lrow_ref[k:k + 1, :], gate_ref[k:k + 1, :], gt)
    tn = (((0,), (0,)), ((), ()))
    moe = lax.dot_general(gt.astype(BF16), y, tn, preferred_element_type=F32)
    xn = xmid_ref[...] + mod_ref[5:6, :] * moe
    if final:
        ms = jnp.mean(xn * xn, axis=-1, keepdims=True)
        xn = xn * lax.rsqrt(ms + NORM_EPS) * fg_ref[...]
    out_ref[...] = xn


def _combine(csrc, xmid, mod_l, lrow, gate, fg, ys, seq, final):
    t, d = xmid.shape
    n_tiles = t // TM
    tpb = seq // TM
    half = ys.shape[1]
    ys3 = ys.reshape(ys.shape[0] // CHUNK, CHUNK, half)
    grid_spec = pltpu.PrefetchScalarGridSpec(
        num_scalar_prefetch=1,
        grid=(n_tiles,),
        in_specs=[
            pl.BlockSpec((TM, d), lambda i, cs: (i, 0)),
            pl.BlockSpec((None, 6, d), lambda i, cs: (i // tpb, 0, 0)),
            pl.BlockSpec((TOP_K, TM), lambda i, cs: (0, i)),
            pl.BlockSpec((TOP_K, TM), lambda i, cs: (0, i)),
            pl.BlockSpec((1, d), lambda i, cs: (0, 0)),
            pl.BlockSpec(memory_space=pl.ANY),
        ],
        out_specs=pl.BlockSpec((TM, d), lambda i, cs: (i, 0)),
        scratch_shapes=[
            pltpu.VMEM((2, CPT, CHUNK, half), U32),
            pltpu.SemaphoreType.DMA((2,)),
        ],
    )
    return pl.pallas_call(
        functools.partial(_combine_kernel, n_tiles=n_tiles, final=final),
        grid_spec=grid_spec,
        out_shape=jax.ShapeDtypeStruct((t, d), F32),
        compiler_params=_cparams(1),
        name="combine",
    )(csrc, xmid, mod_l, lrow, gate, fg, ys3)


def _rope_tables(seq):
    inv = ROPE_THETA ** (-jnp.arange(0, ROPE_DIM, 2, dtype=F32) / ROPE_DIM)
    ang = jnp.arange(seq, dtype=F32)[:, None] * inv[None, :]
    cos, sin = jnp.cos(ang), jnp.sin(ang)
    one = jnp.ones((seq, HEAD_DIM - ROPE_DIM), F32)
    zero8 = jnp.zeros((seq, ROPE_HALF), F32)
    zero48 = jnp.zeros((seq, HEAD_DIM - ROPE_DIM), F32)
    rc = jnp.concatenate([cos, cos, one], axis=1)
    rsa = jnp.concatenate([-sin, zero8, zero48], axis=1)
    rsb = jnp.concatenate([zero8, sin, zero48], axis=1)
    return tuple(jnp.tile(a, (1, LANES // HEAD_DIM)) for a in (rc, rsa, rsb))


def _qkvu_columns():
    cols = []
    for g in range(N_GROUPS):
        for part in range(3):
            cols.append(part * ATTN_WIDTH + g * GROUP_WIDTH + np.arange(GROUP_WIDTH))
    cols.append(3 * ATTN_WIDTH + np.arange(POOL_WIDTH))
    return np.concatenate(cols)


def kernel(x, c, ada_w, ada_b, norm1_g, w_in, pool_w, pool_scale, w_attn_branch,
           w_pool_branch, w_out, norm2_g, router_w, router_b, w_gu, b_gu, w_down, b_down,
           final_g):
    batch, seq, d = x.shape
    depth = ada_w.shape[0]
    t = batch * seq
    assert seq % TM == 0 and t % TM == 0
    n_tiles = t // TM
    n_blocks = -(-(n_tiles * MAX_USED_CHUNKS * CHUNK + N_EXPERTS * (RB - CHUNK)) // RB)

    mod = _ada(c, ada_w, ada_b).reshape(depth, batch, 6, d)
    rope = _rope_tables(seq)
    triu = (np.arange(TM)[:, None] < np.arange(TM)[None, :]).astype(np.float32)
    triu = jnp.asarray(triu, BF16)
    qkvu_cols = _qkvu_columns()
    gate_col0 = 3 * ATTN_WIDTH + POOL_WIDTH
    fg = final_g.reshape(1, d)

    xt = x.reshape(t, d)
    for l in range(depth):
        w_qkvu = w_in[l][:, qkvu_cols].astype(BF16)
        w_gate = w_in[l][:, gate_col0:].astype(BF16)
        rwt = router_w[l].T
        rwh = rwt.astype(BF16)
        rwl = (rwt - rwh.astype(F32)).astype(BF16)
        g1n = norm1_g[l].reshape(1, d)
        g2n = norm2_g[l].reshape(1, d)

        q0, q1, q2, u = _inproj(xt, mod[l], g1n, w_qkvu, rope, batch, seq)
        attn_outs = []
        for qkv, dil in zip((q0, q1, q2), DILATIONS):
            attn_outs.extend(_attention(qkv, dil))
        pool = _pool(u, pool_w[l].astype(BF16), pool_scale[l].reshape(1, POOL_WIDTH),
                     batch, seq)
        xmid, xs, lrow, gate, cmeta = _epilogue(
            xt, mod[l], g1n, w_gate, attn_outs, pool,
            w_attn_branch[l].astype(BF16), w_pool_branch[l].astype(BF16),
            w_out[l].astype(BF16), g2n, rwh, rwl, router_b[l].reshape(N_EXPERTS, 1), triu,
            batch, seq)
        src, bexp, nact, csrc = _route(cmeta[:, 0, :, 0], cmeta[:, 1, :, 0], n_blocks)
        ys = _experts(xs, src, bexp, nact, w_gu[l].astype(BF16), b_gu[l],
                      w_down[l].astype(BF16), b_down[l], n_blocks)
        xt = _combine(csrc, xmid, mod[l], lrow, gate, fg, ys, seq, final=(l == depth - 1))
    return xt.reshape(batch, seq, d)
```

```python
import functools

import numpy as np
import jax
import jax.numpy as jnp
from jax import lax
from jax.experimental import pallas as pl
from jax.experimental.pallas import tpu as pltpu

F32 = jnp.float32
BF16 = jnp.bfloat16
I32 = jnp.int32
U32 = jnp.uint32

HEAD_DIM = 64
HEADS_PER_GROUP = 4
DILATIONS = (1, 4, 16)
N_SIDE = 64
N_GROUPS = len(DILATIONS)
GROUP_WIDTH = HEADS_PER_GROUP * HEAD_DIM
ATTN_WIDTH = N_GROUPS * GROUP_WIDTH
ROPE_DIM = HEAD_DIM // 4
ROPE_HALF = ROPE_DIM // 2
ROPE_THETA = 500000.0
MASK_VALUE = -1e30
POOL_WINDOWS = (2, 4, 8, 16)
POOL_WIDTH = 512
N_EXPERTS = 32
TOP_K = 4
SWIGLU_ALPHA = 1.702
SWIGLU_LIMIT = 7.0
NORM_EPS = 1e-5

LANES = 128
SUBLANES = 8
VMEM_LIMIT = 56 * 1024 * 1024

TM = 512
CHUNK = SUBLANES
RB = 512
BQ = 128
ATTN_ITEMS = 4
POOL_TILE = 256
POOL_HALO = 16


def _round_up(a, m):
    return (a + m - 1) // m * m


R_T = _round_up(TOP_K * TM + N_EXPERTS * (CHUNK - 1), 256)
CPT = R_T // CHUNK
CPB = RB // CHUNK
MAX_USED_CHUNKS = (TOP_K * TM + N_EXPERTS * (CHUNK - 1)) // CHUNK
assert MAX_USED_CHUNKS < CPT
ZERO_CHUNK = CPT - 1


def _cparams(n_axes):
    return pltpu.CompilerParams(
        dimension_semantics=("arbitrary",) * n_axes, vmem_limit_bytes=VMEM_LIMIT)


def _norm_mod(x, g, scale, shift):
    ms = jnp.mean(x * x, axis=-1, keepdims=True)
    y = x * lax.rsqrt(ms + NORM_EPS) * g
    return y * (1.0 + scale) + shift


def _pack_bf16_pairs(v):
    w = v.shape[1] // 2
    bits = pltpu.bitcast(v, U32)
    return (bits[:, :w] >> 16) | (bits[:, w:] & jnp.uint32(0xFFFF0000))


def _unpack_bf16_pairs(p):
    lo = pltpu.bitcast(p << 16, F32).astype(BF16)
    hi = pltpu.bitcast(p & jnp.uint32(0xFFFF0000), F32).astype(BF16)
    return jnp.concatenate([lo, hi], axis=1)


def _ada_kernel(c_ref, w_ref, b_ref, o_ref):
    c = c_ref[...]
    ca = c * jax.nn.sigmoid(c)
    o_ref[0] = jnp.dot(ca, w_ref[0], preferred_element_type=F32,
                       precision=lax.Precision.HIGHEST) + b_ref[0]


def _ada(c, ada_w, ada_b):
    depth, d, n = ada_w.shape
    b = c.shape[0]
    tn = 1536
    return pl.pallas_call(
        _ada_kernel,
        grid=(depth, n // tn),
        in_specs=[
            pl.BlockSpec((b, d), lambda l, j: (0, 0)),
            pl.BlockSpec((1, d, tn), lambda l, j: (l, 0, j)),
            pl.BlockSpec((1, 1, tn), lambda l, j: (l, 0, j)),
        ],
        out_specs=pl.BlockSpec((1, b, tn), lambda l, j: (l, 0, j)),
        out_shape=jax.ShapeDtypeStruct((depth, b, n), F32),
        compiler_params=_cparams(2),
        name="ada",
    )(c, ada_w, ada_b.reshape(depth, 1, n))


def _pack_two_slabs(lo, hi):
    lo_bits = pltpu.bitcast(lo.astype(BF16).astype(F32), U32)
    hi_bits = pltpu.bitcast(hi.astype(BF16).astype(F32), U32)
    return (lo_bits >> 16) | (hi_bits & jnp.uint32(0xFFFF0000))


def _unpack_two_slabs(p):
    lo = pltpu.bitcast(p << 16, F32).astype(BF16)
    hi = pltpu.bitcast(p & jnp.uint32(0xFFFF0000), F32).astype(BF16)
    return lo, hi


def _inproj_kernel(x_ref, mod_ref, g_ref, w_ref, rc_ref, rsa_ref, rsb_ref, qkv_ref, u_ref):
    h = _norm_mod(x_ref[...], g_ref[...], mod_ref[1:2, :], mod_ref[0:1, :])
    hb = h.astype(BF16)
    rc, rsa, rsb = rc_ref[...], rsa_ref[...], rsb_ref[...]

    def rope(s):
        return (s * rc + pltpu.roll(s, LANES - ROPE_HALF, 1) * rsa
                + pltpu.roll(s, ROPE_HALF, 1) * rsb)

    for part in range(3):
        pp = jnp.dot(hb, w_ref[:, part * ATTN_WIDTH:(part + 1) * ATTN_WIDTH],
                     preferred_element_type=F32)
        for g in range(N_GROUPS):
            slabs = []
            for pair in range(2):
                c0 = g * GROUP_WIDTH + pair * LANES
                s = pp[:, c0:c0 + LANES]
                if part < 2:
                    s = rope(s)
                if part == 0:
                    s = s * (HEAD_DIM ** -0.5)
                slabs.append(s)
            qkv_ref[3 * g + part] = _pack_two_slabs(*slabs)
    u_ref[...] = jnp.dot(hb, w_ref[:, 3 * ATTN_WIDTH:], preferred_element_type=F32)


def _inproj(xt, mod_l, g1n, w_qkvu, rope, batch, seq):
    t, d = xt.shape
    tpb = seq // TM
    rc, rsa, rsb = rope
    row = lambda i: (i, 0)
    const = lambda i: (0, 0)
    rope_spec = pl.BlockSpec((TM, LANES), lambda i: (i % tpb, 0))
    return pl.pallas_call(
        _inproj_kernel,
        grid=(t // TM,),
        in_specs=[
            pl.BlockSpec((TM, d), row),
            pl.BlockSpec((None, 6, d), lambda i: (i // tpb, 0, 0)),
            pl.BlockSpec((1, d), const),
            pl.BlockSpec(w_qkvu.shape, const),
            rope_spec, rope_spec, rope_spec,
        ],
        out_specs=[
            pl.BlockSpec((None, 3 * N_GROUPS, TM, LANES), lambda i: (i // tpb, 0, i % tpb, 0)),
            pl.BlockSpec((TM, POOL_WIDTH), row),
        ],
        out_shape=[jax.ShapeDtypeStruct((batch, 3 * N_GROUPS, seq, LANES), U32),
                   jax.ShapeDtypeStruct((t, POOL_WIDTH), F32)],
        compiler_params=_cparams(1),
        name="inproj",
    )(xt, mod_l, g1n, w_qkvu, rc, rsa, rsb)


def _band_bias(bq, kw, off):
    rel = (lax.broadcasted_iota(I32, (bq, kw), 1) - lax.broadcasted_iota(I32, (bq, kw), 0) + off)
    return jnp.where(jnp.abs(rel) <= N_SIDE, 0.0, MASK_VALUE)


def _attn_kernel(qkv_ref, o_ref, lse_ref, bias_ref, *, length, dil):
    bq = min(BQ, length)
    kw = min(length, bq + 2 * N_SIDE)
    nblk = length // bq
    is_a = lax.broadcasted_iota(I32, (1, LANES), 1) < HEAD_DIM
    bias_ref[0] = _band_bias(bq, kw, 0)
    bias_ref[1] = _band_bias(bq, kw, -N_SIDE)
    bias_ref[2] = _band_bias(bq, kw, bq - kw)

    def rows(r, m, n):
        if dil == 1:
            return pl.ds(m, n)
        return pl.ds(r + dil * m, n, stride=dil)

    def step(items):
        qs, ks, vs, bs = [], [], [], []
        for r, i in items:
            m0 = i * bq
            start = jnp.clip(m0 - N_SIDE, 0, length - kw)
            kind = jnp.where(i == 0, 0, jnp.where(i == nblk - 1, 2, 1))
            q_lo, q_hi = _unpack_two_slabs(qkv_ref[0, rows(r, m0, bq), :])
            k_lo, k_hi = _unpack_two_slabs(qkv_ref[1, rows(r, start, kw), :])
            v_lo, v_hi = _unpack_two_slabs(qkv_ref[2, rows(r, start, kw), :])
            bias = bias_ref[kind]
            for q2, k2, v2 in ((q_lo, k_lo, v_lo), (q_hi, k_hi, v_hi)):
                zero = jnp.zeros_like(q2)
                qs += [jnp.where(is_a, q2, zero), jnp.where(is_a, zero, q2)]
                ks += [k2, k2]
                vs += [v2, v2]
                bs += [bias, bias]
        q = jnp.stack(qs)
        k = jnp.stack(ks)
        v = jnp.stack(vs)
        s = jnp.einsum("nqd,nkd->nqk", q, k, preferred_element_type=F32) + jnp.stack(bs)
        mx = jnp.max(s, axis=-1, keepdims=True)
        e = jnp.exp(s - mx)
        den = jnp.sum(e, axis=-1, keepdims=True)
        o = jnp.einsum("nqk,nkd->nqd", e.astype(BF16), v, preferred_element_type=F32) / den
        lse = mx + jnp.log(den)
        for n, (r, i) in enumerate(items):
            for p in range(2):
                c = 4 * n + 2 * p
                dst = rows(r, i * bq, bq)
                o_ref[p, dst, :] = jnp.where(is_a, o[c], o[c + 1])
                lse_ref[p, dst, :] = jnp.where(is_a, lse[c], lse[c + 1])

    if nblk >= ATTN_ITEMS:
        per_res = nblk // ATTN_ITEMS

        def body(t, carry):
            r = t // per_res
            i0 = (t % per_res) * ATTN_ITEMS
            step([(r, i0 + n) for n in range(ATTN_ITEMS)])
            return carry
        lax.fori_loop(0, dil * per_res, body, 0)
    else:
        assert nblk == 1 and dil % ATTN_ITEMS == 0

        def body(t, carry):
            step([(t * ATTN_ITEMS + n, 0) for n in range(ATTN_ITEMS)])
            return carry
        lax.fori_loop(0, dil // ATTN_ITEMS, body, 0)


def _attention(qkv, group):
    batch, _, seq, _ = qkv.shape
    dil = DILATIONS[group]
    length = seq // dil
    bq = min(BQ, length)
    kw = min(length, bq + 2 * N_SIDE)
    out_shape = jax.ShapeDtypeStruct((batch, 2, seq, LANES), F32)
    out_spec = pl.BlockSpec((None, 2, seq, LANES), lambda b: (b, 0, 0, 0))
    return pl.pallas_call(
        functools.partial(_attn_kernel, length=length, dil=dil),
        grid=(batch,),
        in_specs=[pl.BlockSpec((None, 3, seq, LANES), lambda b: (b, group, 0, 0))],
        out_specs=[out_spec, out_spec],
        out_shape=[out_shape, out_shape],
        scratch_shapes=[pltpu.VMEM((3, bq, kw), F32)],
        compiler_params=_cparams(1),
        name=f"attn_d{dil}",
    )(qkv)


def _pool_kernel(u_ref, pw_ref, ps_ref, o_ref, *, seq):
    win = POOL_TILE + 2 * POOL_HALO
    rel0 = (lax.broadcasted_iota(I32, (POOL_TILE, win), 1)
            - lax.broadcasted_iota(I32, (POOL_TILE, win), 0))
    row = lax.broadcasted_iota(I32, (POOL_TILE, 1), 0)

    def tile(i, carry):
        t0 = pl.multiple_of(i * POOL_TILE, POOL_TILE)
        start = pl.multiple_of(jnp.clip(t0 - POOL_HALO, 0, seq - win), SUBLANES)
        rel = rel0 + (start - t0)
        pos = row + t0
        for g, w in enumerate(POOL_WINDOWS):
            half = w // 2
            cs = slice(g * LANES, (g + 1) * LANES)
            band = jnp.where(jnp.abs(2 * rel + 1) < w, 1.0, 0.0).astype(BF16)
            uw = u_ref[pl.ds(start, win), cs]
            hi = uw.astype(BF16)
            lo = (uw - hi.astype(F32)).astype(BF16)
            wsum = (jnp.dot(band, hi, preferred_element_type=F32)
                    + jnp.dot(band, lo, preferred_element_type=F32))
            cnt = (jnp.minimum(pos + half, seq) - jnp.maximum(pos - half, 0)).astype(F32)
            diff = wsum / cnt - u_ref[pl.ds(t0, POOL_TILE), cs]
            y = jnp.dot(diff.astype(BF16), pw_ref[g], preferred_element_type=F32)
            o_ref[pl.ds(t0, POOL_TILE), cs] = (y * ps_ref[:, cs]).astype(BF16)
        return carry

    lax.fori_loop(0, seq // POOL_TILE, tile, 0)


def _pool(u, pool_w, pool_scale, batch, seq):
    t = u.shape[0]
    return pl.pallas_call(
        functools.partial(_pool_kernel, seq=seq),
        grid=(batch,),
        in_specs=[
            pl.BlockSpec((seq, POOL_WIDTH), lambda b: (b, 0)),
            pl.BlockSpec(pool_w.shape, lambda b: (0, 0, 0)),
            pl.BlockSpec((1, POOL_WIDTH), lambda b: (0, 0)),
        ],
        out_specs=pl.BlockSpec((seq, POOL_WIDTH), lambda b: (b, 0)),
        out_shape=jax.ShapeDtypeStruct((t, POOL_WIDTH), BF16),
        compiler_params=_cparams(1),
        name="pool",
    )(u, pool_w, pool_scale)


def _epilogue_kernel(x_ref, mod_ref, g1n_ref, wg_ref,
                     o0_ref, l0_ref, o1_ref, l1_ref, o2_ref, l2_ref,
                     pool_ref, wa_ref, wp_ref, wo_ref, g2n_ref,
                     rwh_ref, rwl_ref, rb_ref, triu_ref,
                     xmid_ref, xs_ref, lrow_ref, gate_ref, cmeta_ref):
    d = x_ref.shape[1]
    x = x_ref[...]
    sh1, sc1, g1 = mod_ref[0:1, :], mod_ref[1:2, :], mod_ref[2:3, :]
    sh2, sc2 = mod_ref[3:4, :], mod_ref[4:5, :]

    hb = _norm_mod(x, g1n_ref[...], sc1, sh1).astype(BF16)
    gates = jax.nn.sigmoid(jnp.dot(hb, wg_ref[...], preferred_element_type=F32))
    halves = []
    for p in range(2):
        a0, a1, a2 = l0_ref[p], l1_ref[p], l2_ref[p]
        mx = jnp.maximum(jnp.maximum(a0, a1), a2)
        w0, w1, w2 = jnp.exp(a0 - mx), jnp.exp(a1 - mx), jnp.exp(a2 - mx)
        num = w0 * o0_ref[p] + w1 * o1_ref[p] + w2 * o2_ref[p]
        halves.append((num / (w0 + w1 + w2)).astype(BF16))
    attn = jnp.concatenate(halves, axis=1)
    a_br = jnp.dot(attn, wa_ref[...], preferred_element_type=F32)
    p_br = jnp.dot(pool_ref[...], wp_ref[...], preferred_element_type=F32)
    merged = gates[:, :d] * a_br + gates[:, d:] * p_br
    mix = jnp.dot(merged.astype(BF16), wo_ref[...], preferred_element_type=F32)
    xm = x + g1 * mix
    xmid_ref[...] = xm

    h2 = _norm_mod(xm, g2n_ref[...], sc2, sh2)
    h2b = h2.astype(BF16)
    h2l = (h2 - h2b.astype(F32)).astype(BF16)
    nt = (((1,), (1,)), ((), ()))
    rwh = rwh_ref[...]
    logits = (lax.dot_general(rwh, h2b, nt, preferred_element_type=F32)
              + lax.dot_general(rwl_ref[...], h2b, nt, preferred_element_type=F32)
              + lax.dot_general(rwh, h2l, nt, preferred_element_type=F32)
              + rb_ref[...])
    tm = logits.shape[1]

    e_io = lax.broadcasted_iota(I32, (N_EXPERTS, tm), 0)
    work = logits
    sels, vals = [], []
    for _ in range(TOP_K):
        mx = jnp.max(work, axis=0, keepdims=True)
        idx = jnp.min(jnp.where(work == mx, e_io, N_EXPERTS), axis=0, keepdims=True)
        sel = e_io == idx
        sels.append(sel)
        vals.append(mx)
        work = jnp.where(sel, -jnp.inf, work)
    exps = [jnp.exp(v - vals[0]) for v in vals]
    den = exps[0] + exps[1] + exps[2] + exps[3]
    for k in range(TOP_K):
        gate_ref[k:k + 1, :] = exps[k] / den

    triu = triu_ref[...]
    ohs = [jnp.where(s, 1.0, 0.0) for s in sels]
    withins = [jnp.dot(o.astype(BF16), triu, preferred_element_type=F32) for o in ohs]
    cnts = [jnp.sum(o, axis=1, keepdims=True) for o in ohs]
    n_e = cnts[0] + cnts[1] + cnts[2] + cnts[3]
    chunks = jnp.floor((n_e + (CHUNK - 1)) * (1.0 / CHUNK))
    chunks_l = jnp.broadcast_to(chunks, (N_EXPERTS, LANES))
    ltri = jnp.where(lax.broadcasted_iota(I32, (N_EXPERTS, N_EXPERTS), 0)
                     > lax.broadcasted_iota(I32, (N_EXPERTS, N_EXPERTS), 1), 1.0, 0.0)
    rstart_l = jnp.dot(ltri.astype(BF16), chunks_l.astype(BF16), preferred_element_type=F32)
    cmeta_ref[0] = chunks_l.astype(I32)
    cmeta_ref[1] = rstart_l.astype(I32)
    base = rstart_l[:, 0:1] * CHUNK
    lrows = []
    for k in range(TOP_K):
        lr = jnp.sum(ohs[k] * (base + withins[k]), axis=0, keepdims=True).astype(I32)
        lrow_ref[k:k + 1, :] = lr
        lrows.append(lr)
        base = base + cnts[k]

    blk = 256
    for r0 in range(0, R_T, blk):
        r_io = lax.broadcasted_iota(I32, (blk, tm), 0) + r0
        hit = jnp.where(r_io == lrows[0], 1.0,
              jnp.where(r_io == lrows[1], 1.0,
              jnp.where(r_io == lrows[2], 1.0,
              jnp.where(r_io == lrows[3], 1.0, 0.0))))
        rows = jnp.dot(hit.astype(BF16), h2b, preferred_element_type=F32)
        xs_ref[r0:r0 + blk, :] = _pack_bf16_pairs(rows)


def _epilogue(xt, mod_l, g1n, wg, attn_outs, pool, wa, wp, wo, g2n, rwh, rwl, rb, triu,
              batch, seq):
    t, d = xt.shape
    n_tiles = t // TM
    tpb = seq // TM
    row = lambda i: (i, 0)
    const = lambda i: (0, 0)
    half_spec = pl.BlockSpec((None, 2, TM, LANES), lambda i: (i // tpb, 0, i % tpb, 0))
    full = lambda a: pl.BlockSpec(a.shape, const)
    return pl.pallas_call(
        _epilogue_kernel,
        grid=(n_tiles,),
        in_specs=[
            pl.BlockSpec((TM, d), row),
            pl.BlockSpec((None, 6, d), lambda i: (i // tpb, 0, 0)),
            full(g1n), full(wg),
            half_spec, half_spec, half_spec, half_spec, half_spec, half_spec,
            pl.BlockSpec((TM, POOL_WIDTH), row),
            full(wa), full(wp), full(wo), full(g2n),
            full(rwh), full(rwl), full(rb), full(triu),
        ],
        out_specs=[
            pl.BlockSpec((TM, d), row),
            pl.BlockSpec((R_T, d // 2), row),
            pl.BlockSpec((TOP_K, TM), lambda i: (0, i)),
            pl.BlockSpec((TOP_K, TM), lambda i: (0, i)),
            pl.BlockSpec((None, 2, N_EXPERTS, LANES), lambda i: (i, 0, 0, 0)),
        ],
        out_shape=[
            jax.ShapeDtypeStruct((t, d), F32),
            jax.ShapeDtypeStruct((n_tiles * R_T, d // 2), U32),
            jax.ShapeDtypeStruct((TOP_K, t), I32),
            jax.ShapeDtypeStruct((TOP_K, t), F32),
            jax.ShapeDtypeStruct((n_tiles, 2, N_EXPERTS, LANES), I32),
        ],
        compiler_params=_cparams(1),
        name="epilogue",
    )(xt, mod_l, g1n, wg, *attn_outs, pool, wa, wp, wo, g2n, rwh, rwl, rb, triu)


def _route_kernel(cnt_ref, rs_ref, src_ref, bexp_ref, nact_ref, csrc_ref, *, n_tiles, n_blocks):
    def per_expert(e, carry):
        seg, last_e = carry

        def per_tile(j, tot):
            n = cnt_ref[j, e]
            d0 = seg * CPB + tot
            s0 = j * CPT + rs_ref[j, e]

            def per_chunk(i, c):
                src_ref[d0 + i] = s0 + i
                csrc_ref[s0 + i] = d0 + i
                return c
            lax.fori_loop(0, n, per_chunk, 0)
            return tot + n
        tot = lax.fori_loop(0, n_tiles, per_tile, 0)
        nb = lax.shift_right_logical(tot + (CPB - 1), int(np.log2(CPB)))

        def pad_slot(s, c):
            src_ref[s] = ZERO_CHUNK
            return c
        lax.fori_loop(seg * CPB + tot, (seg + nb) * CPB, pad_slot, 0)

        def set_block(b, c):
            bexp_ref[seg + b] = e
            return c
        lax.fori_loop(0, nb, set_block, 0)
        return seg + nb, jnp.where(nb > 0, e, last_e)

    nact, last_e = lax.fori_loop(0, N_EXPERTS, per_expert, (0, 0))
    nact_ref[0] = nact

    def idle_block(b, c):
        bexp_ref[b] = last_e

        def pad_slot(s, c2):
            src_ref[b * CPB + s] = ZERO_CHUNK
            return c2
        lax.fori_loop(0, CPB, pad_slot, 0)
        return c
    lax.fori_loop(nact, n_blocks, idle_block, 0)

    def per_tile_tail(j, c):
        used = lax.fori_loop(0, N_EXPERTS, lambda e, u: u + cnt_ref[j, e], 0)

        def pad_slot(s, c2):
            csrc_ref[j * CPT + s] = 0
            return c2
        lax.fori_loop(used, CPT, pad_slot, 0)
        return c
    lax.fori_loop(0, n_tiles, per_tile_tail, 0)


def _route(cnt, rs, n_blocks):
    n_tiles = cnt.shape[0]
    smem = pl.BlockSpec(memory_space=pltpu.SMEM)
    return pl.pallas_call(
        functools.partial(_route_kernel, n_tiles=n_tiles, n_blocks=n_blocks),
        in_specs=[smem, smem],
        out_specs=[smem, smem, smem, smem],
        out_shape=[
            jax.ShapeDtypeStruct((n_blocks * CPB,), I32),
            jax.ShapeDtypeStruct((n_blocks,), I32),
            jax.ShapeDtypeStruct((1,), I32),
            jax.ShapeDtypeStruct((n_tiles * CPT,), I32),
        ],
        name="route",
    )(cnt, rs)


def _chunk_ring(table_ref, src_hbm, buf, sem, step, n_steps, n_chunks):
    slot = lax.rem(step, 2)

    def start(which, to_slot):
        def issue(i, c):
            pltpu.make_async_copy(src_hbm.at[table_ref[which * n_chunks + i]],
                                  buf.at[to_slot, i], sem.at[to_slot]).start()
            return c
        lax.fori_loop(0, n_chunks, issue, 0)

    @pl.when(step == 0)
    def _():
        start(0, 0)

    @pl.when(step + 1 < n_steps)
    def _():
        start(step + 1, 1 - slot)

    pltpu.make_async_copy(src_hbm.at[pl.ds(0, n_chunks)], buf.at[slot], sem.at[slot]).wait()
    return slot


def _expert_kernel(bexp_ref, nact_ref, src_ref, xs_hbm, wgu_ref, bgu_ref, wd_ref, bd_ref,
                   ys_ref, xbuf, sem, wgu_b, wd_b, *, n_blocks):
    b = pl.program_id(0)
    slot = _chunk_ring(src_ref, xs_hbm, xbuf, sem, b, n_blocks, CPB)
    f = wd_ref.shape[0]

    @pl.when((b == 0) | (bexp_ref[b] != bexp_ref[jnp.maximum(b - 1, 0)]))
    def _():
        wgu_b[...] = wgu_ref[...].astype(BF16)
        wd_b[...] = wd_ref[...].astype(BF16)

    @pl.when(b < nact_ref[0])
    def _():
        x = _unpack_bf16_pairs(xbuf[slot].reshape(RB, xbuf.shape[-1]))
        z = jnp.dot(x, wgu_b[...], preferred_element_type=F32) + bgu_ref[...]
        glu = jnp.minimum(z[:, :f], SWIGLU_LIMIT)
        lin = jnp.clip(z[:, f:], -SWIGLU_LIMIT, SWIGLU_LIMIT)
        act = glu * jax.nn.sigmoid(SWIGLU_ALPHA * glu) * (lin + 1.0)
        y = jnp.dot(act.astype(BF16), wd_b[...], preferred_element_type=F32) + bd_ref[...]
        ys_ref[...] = _pack_bf16_pairs(y.astype(BF16).astype(F32))

    @pl.when(b >= nact_ref[0])
    def _():
        ys_ref[...] = jnp.zeros_like(ys_ref)


def _experts(xs, src, bexp, nact, wgu, bgu, wd, bd, layer, n_blocks):
    half = xs.shape[1]
    _, n_exp, d, f2 = wgu.shape
    f = wd.shape[2]
    xs3 = xs.reshape(xs.shape[0] // CHUNK, CHUNK, half)
    expert = lambda b, be, na, sr: (layer, be[b], 0, 0)
    grid_spec = pltpu.PrefetchScalarGridSpec(
        num_scalar_prefetch=3,
        grid=(n_blocks,),
        in_specs=[
            pl.BlockSpec(memory_space=pl.ANY),
            pl.BlockSpec((None, None, d, f2), expert),
            pl.BlockSpec((None, None, 1, f2), expert),
            pl.BlockSpec((None, None, f, d), expert),
            pl.BlockSpec((None, None, 1, d), expert),
        ],
        out_specs=pl.BlockSpec((RB, half), lambda b, be, na, sr: (b, 0)),
        scratch_shapes=[
            pltpu.VMEM((2, CPB, CHUNK, half), U32),
            pltpu.SemaphoreType.DMA((2,)),
            pltpu.VMEM((d, f2), BF16),
            pltpu.VMEM((f, d), BF16),
        ],
    )
    depth = wgu.shape[0]
    return pl.pallas_call(
        functools.partial(_expert_kernel, n_blocks=n_blocks),
        grid_spec=grid_spec,
        out_shape=jax.ShapeDtypeStruct((n_blocks * RB, half), U32),
        compiler_params=_cparams(1),
        name="experts",
    )(bexp, nact, src, xs3, wgu, bgu.reshape(depth, n_exp, 1, f2), wd,
      bd.reshape(depth, n_exp, 1, d))


def _combine_kernel(csrc_ref, xmid_ref, mod_ref, lrow_ref, gate_ref, fg_ref, ys_hbm,
                    out_ref, ybuf, sem, *, n_tiles, final):
    j = pl.program_id(0)
    slot = _chunk_ring(csrc_ref, ys_hbm, ybuf, sem, j, n_tiles, CPT)
    tm = xmid_ref.shape[0]
    y = _unpack_bf16_pairs(ybuf[slot].reshape(R_T, ybuf.shape[-1]))
    r_io = lax.broadcasted_iota(I32, (R_T, tm), 0)
    gt = jnp.zeros((R_T, tm), F32)
    for k in reversed(range(TOP_K)):
        gt = jnp.where(r_io == lrow_ref[k:k + 1, :], gate_ref[k:k + 1, :], gt)
    tn = (((0,), (0,)), ((), ()))
    moe = lax.dot_general(gt.astype(BF16), y, tn, preferred_element_type=F32)
    xn = xmid_ref[...] + mod_ref[5:6, :] * moe
    if final:
        ms = jnp.mean(xn * xn, axis=-1, keepdims=True)
        xn = xn * lax.rsqrt(ms + NORM_EPS) * fg_ref[...]
    out_ref[...] = xn


def _combine(csrc, xmid, mod_l, lrow, gate, fg, ys, seq, final):
    t, d = xmid.shape
    n_tiles = t // TM
    tpb = seq // TM
    half = ys.shape[1]
    ys3 = ys.reshape(ys.shape[0] // CHUNK, CHUNK, half)
    grid_spec = pltpu.PrefetchScalarGridSpec(
        num_scalar_prefetch=1,
        grid=(n_tiles,),
        in_specs=[
            pl.BlockSpec((TM, d), lambda i, cs: (i, 0)),
            pl.BlockSpec((None, 6, d), lambda i, cs: (i // tpb, 0, 0)),
            pl.BlockSpec((TOP_K, TM), lambda i, cs: (0, i)),
            pl.BlockSpec((TOP_K, TM), lambda i, cs: (0, i)),
            pl.BlockSpec((1, d), lambda i, cs: (0, 0)),
            pl.BlockSpec(memory_space=pl.ANY),
        ],
        out_specs=pl.BlockSpec((TM, d), lambda i, cs: (i, 0)),
        scratch_shapes=[
            pltpu.VMEM((2, CPT, CHUNK, half), U32),
            pltpu.SemaphoreType.DMA((2,)),
        ],
    )
    return pl.pallas_call(
        functools.partial(_combine_kernel, n_tiles=n_tiles, final=final),
        grid_spec=grid_spec,
        out_shape=jax.ShapeDtypeStruct((t, d), F32),
        compiler_params=_cparams(1),
        name="combine",
    )(csrc, xmid, mod_l, lrow, gate, fg, ys3)


def _rope_tables(seq):
    inv = ROPE_THETA ** (-jnp.arange(0, ROPE_DIM, 2, dtype=F32) / ROPE_DIM)
    ang = jnp.arange(seq, dtype=F32)[:, None] * inv[None, :]
    cos, sin = jnp.cos(ang), jnp.sin(ang)
    one = jnp.ones((seq, HEAD_DIM - ROPE_DIM), F32)
    zero8 = jnp.zeros((seq, ROPE_HALF), F32)
    zero48 = jnp.zeros((seq, HEAD_DIM - ROPE_DIM), F32)
    rc = jnp.concatenate([cos, cos, one], axis=1)
    rsa = jnp.concatenate([-sin, zero8, zero48], axis=1)
    rsb = jnp.concatenate([zero8, sin, zero48], axis=1)
    return tuple(jnp.tile(a, (1, LANES // HEAD_DIM)) for a in (rc, rsa, rsb))


def kernel(x, c, ada_w, ada_b, norm1_g, w_in, pool_w, pool_scale, w_attn_branch,
           w_pool_branch, w_out, norm2_g, router_w, router_b, w_gu, b_gu, w_down, b_down,
           final_g):
    batch, seq, d = x.shape
    depth = ada_w.shape[0]
    t = batch * seq
    assert seq % TM == 0 and t % TM == 0
    n_tiles = t // TM
    n_blocks = -(-(n_tiles * MAX_USED_CHUNKS * CHUNK + N_EXPERTS * (RB - CHUNK)) // RB)

    mod = _ada(c, ada_w, ada_b).reshape(depth, batch, 6, d)
    rope = _rope_tables(seq)
    triu = (np.arange(TM)[:, None] < np.arange(TM)[None, :]).astype(np.float32)
    triu = jnp.asarray(triu, BF16)
    gate_col0 = 3 * ATTN_WIDTH + POOL_WIDTH
    fg = final_g.reshape(1, d)

    xt = x.reshape(t, d)
    for l in range(depth):
        w_qkvu = w_in[l][:, :gate_col0].astype(BF16)
        w_gate = w_in[l][:, gate_col0:].astype(BF16)
        rwt = router_w[l].T
        rwh = rwt.astype(BF16)
        rwl = (rwt - rwh.astype(F32)).astype(BF16)
        g1n = norm1_g[l].reshape(1, d)
        g2n = norm2_g[l].reshape(1, d)

        qkv, u = _inproj(xt, mod[l], g1n, w_qkvu, rope, batch, seq)
        attn_outs = []
        for group in range(N_GROUPS):
            attn_outs.extend(_attention(qkv, group))
        pool = _pool(u, pool_w[l].astype(BF16), pool_scale[l].reshape(1, POOL_WIDTH),
                     batch, seq)
        xmid, xs, lrow, gate, cmeta = _epilogue(
            xt, mod[l], g1n, w_gate, attn_outs, pool,
            w_attn_branch[l].astype(BF16), w_pool_branch[l].astype(BF16),
            w_out[l].astype(BF16), g2n, rwh, rwl, router_b[l].reshape(N_EXPERTS, 1), triu,
            batch, seq)
        src, bexp, nact, csrc = _route(cmeta[:, 0, :, 0], cmeta[:, 1, :, 0], n_blocks)
        ys = _experts(xs, src, bexp, nact, w_gu, b_gu, w_down, b_down, l, n_blocks)
        xt = _combine(csrc, xmid, mod[l], lrow, gate, fg, ys, seq, final=(l == depth - 1))
    return xt.reshape(batch, seq, d)
```

```python
import functools

import numpy as np
import jax
import jax.numpy as jnp
from jax import lax
from jax.experimental import pallas as pl
from jax.experimental.pallas import tpu as pltpu

F32 = jnp.float32
BF16 = jnp.bfloat16
I32 = jnp.int32
U32 = jnp.uint32

HEAD_DIM = 64
HEADS_PER_GROUP = 4
DILATIONS = (1, 4, 16)
N_SIDE = 64
N_GROUPS = len(DILATIONS)
GROUP_WIDTH = HEADS_PER_GROUP * HEAD_DIM
ATTN_WIDTH = N_GROUPS * GROUP_WIDTH
ROPE_DIM = HEAD_DIM // 4
ROPE_HALF = ROPE_DIM // 2
ROPE_THETA = 500000.0
MASK_VALUE = -1e30
POOL_WINDOWS = (2, 4, 8, 16)
POOL_WIDTH = 512
N_EXPERTS = 32
TOP_K = 4
SWIGLU_ALPHA = 1.702
SWIGLU_LIMIT = 7.0
NORM_EPS = 1e-5

LANES = 128
SUBLANES = 8
VMEM_LIMIT = 56 * 1024 * 1024

TM = 512
CHUNK = SUBLANES
RB = 512
BQ = 128
ATTN_ITEMS = 4
DMA_UNROLL = 8
POOL_TILE = 256
POOL_HALO = 16


def _round_up(a, m):
    return (a + m - 1) // m * m


R_T = _round_up(TOP_K * TM + N_EXPERTS * (CHUNK - 1), 256)
CPT = R_T // CHUNK
CPB = RB // CHUNK
MAX_USED_CHUNKS = (TOP_K * TM + N_EXPERTS * (CHUNK - 1)) // CHUNK
assert MAX_USED_CHUNKS < CPT
ZERO_CHUNK = CPT - 1


def _cparams(n_axes):
    return pltpu.CompilerParams(
        dimension_semantics=("arbitrary",) * n_axes, vmem_limit_bytes=VMEM_LIMIT)


def _norm_mod(x, g, scale, shift):
    ms = jnp.mean(x * x, axis=-1, keepdims=True)
    y = x * lax.rsqrt(ms + NORM_EPS) * g
    return y * (1.0 + scale) + shift


def _pack_bf16_pairs(v):
    w = v.shape[1] // 2
    bits = pltpu.bitcast(v, U32)
    return (bits[:, :w] >> 16) | (bits[:, w:] & jnp.uint32(0xFFFF0000))


def _unpack_bf16_pairs(p):
    lo = pltpu.bitcast(p << 16, F32).astype(BF16)
    hi = pltpu.bitcast(p & jnp.uint32(0xFFFF0000), F32).astype(BF16)
    return jnp.concatenate([lo, hi], axis=1)


def _ada_kernel(c_ref, w_ref, b_ref, o_ref):
    c = c_ref[...]
    ca = c * jax.nn.sigmoid(c)
    o_ref[0] = jnp.dot(ca, w_ref[0], preferred_element_type=F32,
                       precision=lax.Precision.HIGHEST) + b_ref[0]


def _ada(c, ada_w, ada_b):
    depth, d, n = ada_w.shape
    b = c.shape[0]
    tn = 1536
    return pl.pallas_call(
        _ada_kernel,
        grid=(depth, n // tn),
        in_specs=[
            pl.BlockSpec((b, d), lambda l, j: (0, 0)),
            pl.BlockSpec((1, d, tn), lambda l, j: (l, 0, j)),
            pl.BlockSpec((1, 1, tn), lambda l, j: (l, 0, j)),
        ],
        out_specs=pl.BlockSpec((1, b, tn), lambda l, j: (l, 0, j)),
        out_shape=jax.ShapeDtypeStruct((depth, b, n), F32),
        compiler_params=_cparams(2),
        name="ada",
    )(c, ada_w, ada_b.reshape(depth, 1, n))


def _pack_two_slabs(lo, hi):
    lo_bits = pltpu.bitcast(lo.astype(BF16).astype(F32), U32)
    hi_bits = pltpu.bitcast(hi.astype(BF16).astype(F32), U32)
    return (lo_bits >> 16) | (hi_bits & jnp.uint32(0xFFFF0000))


def _unpack_two_slabs(p):
    lo = pltpu.bitcast(p << 16, F32).astype(BF16)
    hi = pltpu.bitcast(p & jnp.uint32(0xFFFF0000), F32).astype(BF16)
    return lo, hi


def _inproj_kernel(x_ref, mod_ref, g_ref, w_ref, rc_ref, rsa_ref, rsb_ref, qkv_ref, u_ref):
    h = _norm_mod(x_ref[...], g_ref[...], mod_ref[1:2, :], mod_ref[0:1, :])
    hb = h.astype(BF16)
    rc, rsa, rsb = rc_ref[...], rsa_ref[...], rsb_ref[...]

    def rope(s):
        return (s * rc + pltpu.roll(s, LANES - ROPE_HALF, 1) * rsa
                + pltpu.roll(s, ROPE_HALF, 1) * rsb)

    for part in range(3):
        pp = jnp.dot(hb, w_ref[:, part * ATTN_WIDTH:(part + 1) * ATTN_WIDTH],
                     preferred_element_type=F32)
        for g in range(N_GROUPS):
            slabs = []
            for pair in range(2):
                c0 = g * GROUP_WIDTH + pair * LANES
                s = pp[:, c0:c0 + LANES]
                if part < 2:
                    s = rope(s)
                if part == 0:
                    s = s * (HEAD_DIM ** -0.5)
                slabs.append(s)
            qkv_ref[3 * g + part] = _pack_two_slabs(*slabs)
    u_ref[...] = jnp.dot(hb, w_ref[:, 3 * ATTN_WIDTH:], preferred_element_type=F32)


def _inproj(xt, mod_l, g1n, w_qkvu, rope, batch, seq):
    t, d = xt.shape
    tpb = seq // TM
    rc, rsa, rsb = rope
    row = lambda i: (i, 0)
    const = lambda i: (0, 0)
    rope_spec = pl.BlockSpec((TM, LANES), lambda i: (i % tpb, 0))
    return pl.pallas_call(
        _inproj_kernel,
        grid=(t // TM,),
        in_specs=[
            pl.BlockSpec((TM, d), row),
            pl.BlockSpec((None, 6, d), lambda i: (i // tpb, 0, 0)),
            pl.BlockSpec((1, d), const),
            pl.BlockSpec(w_qkvu.shape, const),
            rope_spec, rope_spec, rope_spec,
        ],
        out_specs=[
            pl.BlockSpec((None, 3 * N_GROUPS, TM, LANES), lambda i: (i // tpb, 0, i % tpb, 0)),
            pl.BlockSpec((TM, POOL_WIDTH), row),
        ],
        out_shape=[jax.ShapeDtypeStruct((batch, 3 * N_GROUPS, seq, LANES), U32),
                   jax.ShapeDtypeStruct((t, POOL_WIDTH), F32)],
        compiler_params=_cparams(1),
        name="inproj",
    )(xt, mod_l, g1n, w_qkvu, rc, rsa, rsb)


def _band_bias(bq, kw, off):
    rel = (lax.broadcasted_iota(I32, (bq, kw), 1) - lax.broadcasted_iota(I32, (bq, kw), 0) + off)
    return jnp.where(jnp.abs(rel) <= N_SIDE, 0.0, MASK_VALUE)


def _attn_kernel(qkv_ref, o_ref, lse_ref, bias_ref, *, length, dil):
    bq = min(BQ, length)
    kw = min(length, bq + 2 * N_SIDE)
    nblk = length // bq
    is_a = lax.broadcasted_iota(I32, (1, LANES), 1) < HEAD_DIM
    bias_ref[0] = _band_bias(bq, kw, 0)
    bias_ref[1] = _band_bias(bq, kw, -N_SIDE)
    bias_ref[2] = _band_bias(bq, kw, bq - kw)

    def rows(r, m, n):
        if dil == 1:
            return pl.ds(m, n)
        return pl.ds(r + dil * m, n, stride=dil)

    def step(items):
        qs, ks, vs, bs = [], [], [], []
        for r, i in items:
            m0 = i * bq
            start = jnp.clip(m0 - N_SIDE, 0, length - kw)
            kind = jnp.where(i == 0, 0, jnp.where(i == nblk - 1, 2, 1))
            q_lo, q_hi = _unpack_two_slabs(qkv_ref[0, rows(r, m0, bq), :])
            k_lo, k_hi = _unpack_two_slabs(qkv_ref[1, rows(r, start, kw), :])
            v_lo, v_hi = _unpack_two_slabs(qkv_ref[2, rows(r, start, kw), :])
            bias = bias_ref[kind]
            for q2, k2, v2 in ((q_lo, k_lo, v_lo), (q_hi, k_hi, v_hi)):
                zero = jnp.zeros_like(q2)
                qs += [jnp.where(is_a, q2, zero), jnp.where(is_a, zero, q2)]
                ks += [k2, k2]
                vs += [v2, v2]
                bs += [bias, bias]
        q = jnp.stack(qs)
        k = jnp.stack(ks)
        v = jnp.stack(vs)
        s = jnp.einsum("nqd,nkd->nqk", q, k, preferred_element_type=F32) + jnp.stack(bs)
        mx = jnp.max(s, axis=-1, keepdims=True)
        e = jnp.exp(s - mx)
        den = jnp.sum(e, axis=-1, keepdims=True)
        o = jnp.einsum("nqk,nkd->nqd", e.astype(BF16), v, preferred_element_type=F32) / den
        lse = mx + jnp.log(den)
        for n, (r, i) in enumerate(items):
            for p in range(2):
                c = 4 * n + 2 * p
                dst = rows(r, i * bq, bq)
                o_ref[p, dst, :] = jnp.where(is_a, o[c], o[c + 1])
                lse_ref[p, dst, :] = jnp.where(is_a, lse[c], lse[c + 1])

    if nblk >= ATTN_ITEMS:
        per_res = nblk // ATTN_ITEMS

        def body(t, carry):
            r = t // per_res
            i0 = (t % per_res) * ATTN_ITEMS
            step([(r, i0 + n) for n in range(ATTN_ITEMS)])
            return carry
        lax.fori_loop(0, dil * per_res, body, 0)
    else:
        assert nblk == 1 and dil % ATTN_ITEMS == 0

        def body(t, carry):
            step([(t * ATTN_ITEMS + n, 0) for n in range(ATTN_ITEMS)])
            return carry
        lax.fori_loop(0, dil // ATTN_ITEMS, body, 0)


def _attention(qkv, group):
    batch, _, seq, _ = qkv.shape
    dil = DILATIONS[group]
    length = seq // dil
    bq = min(BQ, length)
    kw = min(length, bq + 2 * N_SIDE)
    out_shape = jax.ShapeDtypeStruct((batch, 2, seq, LANES), F32)
    out_spec = pl.BlockSpec((None, 2, seq, LANES), lambda b: (b, 0, 0, 0))
    return pl.pallas_call(
        functools.partial(_attn_kernel, length=length, dil=dil),
        grid=(batch,),
        in_specs=[pl.BlockSpec((None, 3, seq, LANES), lambda b: (b, group, 0, 0))],
        out_specs=[out_spec, out_spec],
        out_shape=[out_shape, out_shape],
        scratch_shapes=[pltpu.VMEM((3, bq, kw), F32)],
        compiler_params=_cparams(1),
        name=f"attn_d{dil}",
    )(qkv)


def _pool_kernel(u_ref, pw_ref, ps_ref, o_ref, *, seq):
    win = POOL_TILE + 2 * POOL_HALO
    rel0 = (lax.broadcasted_iota(I32, (POOL_TILE, win), 1)
            - lax.broadcasted_iota(I32, (POOL_TILE, win), 0))
    row = lax.broadcasted_iota(I32, (POOL_TILE, 1), 0)

    def tile(i, carry):
        t0 = pl.multiple_of(i * POOL_TILE, POOL_TILE)
        start = pl.multiple_of(jnp.clip(t0 - POOL_HALO, 0, seq - win), SUBLANES)
        rel = rel0 + (start - t0)
        pos = row + t0
        for g, w in enumerate(POOL_WINDOWS):
            half = w // 2
            cs = slice(g * LANES, (g + 1) * LANES)
            band = jnp.where(jnp.abs(2 * rel + 1) < w, 1.0, 0.0).astype(BF16)
            uw = u_ref[pl.ds(start, win), cs]
            hi = uw.astype(BF16)
            lo = (uw - hi.astype(F32)).astype(BF16)
            wsum = (jnp.dot(band, hi, preferred_element_type=F32)
                    + jnp.dot(band, lo, preferred_element_type=F32))
            cnt = (jnp.minimum(pos + half, seq) - jnp.maximum(pos - half, 0)).astype(F32)
            diff = wsum / cnt - u_ref[pl.ds(t0, POOL_TILE), cs]
            y = jnp.dot(diff.astype(BF16), pw_ref[g], preferred_element_type=F32)
            o_ref[pl.ds(t0, POOL_TILE), cs] = (y * ps_ref[:, cs]).astype(BF16)
        return carry

    lax.fori_loop(0, seq // POOL_TILE, tile, 0)


def _pool(u, pool_w, pool_scale, batch, seq):
    t = u.shape[0]
    return pl.pallas_call(
        functools.partial(_pool_kernel, seq=seq),
        grid=(batch,),
        in_specs=[
            pl.BlockSpec((seq, POOL_WIDTH), lambda b: (b, 0)),
            pl.BlockSpec(pool_w.shape, lambda b: (0, 0, 0)),
            pl.BlockSpec((1, POOL_WIDTH), lambda b: (0, 0)),
        ],
        out_specs=pl.BlockSpec((seq, POOL_WIDTH), lambda b: (b, 0)),
        out_shape=jax.ShapeDtypeStruct((t, POOL_WIDTH), BF16),
        compiler_params=_cparams(1),
        name="pool",
    )(u, pool_w, pool_scale)


def _epilogue_kernel(x_ref, mod_ref, g1n_ref, wg_ref,
                     o0_ref, l0_ref, o1_ref, l1_ref, o2_ref, l2_ref,
                     pool_ref, wa_ref, wp_ref, wo_ref, g2n_ref,
                     rwh_ref, rwl_ref, rb_ref, triu_ref,
                     xmid_ref, xs_ref, lrow_ref, gate_ref, cmeta_ref):
    d = x_ref.shape[1]
    x = x_ref[...]
    sh1, sc1, g1 = mod_ref[0:1, :], mod_ref[1:2, :], mod_ref[2:3, :]
    sh2, sc2 = mod_ref[3:4, :], mod_ref[4:5, :]

    hb = _norm_mod(x, g1n_ref[...], sc1, sh1).astype(BF16)
    gates = jax.nn.sigmoid(jnp.dot(hb, wg_ref[...], preferred_element_type=F32))
    halves = []
    for p in range(2):
        a0, a1, a2 = l0_ref[p], l1_ref[p], l2_ref[p]
        mx = jnp.maximum(jnp.maximum(a0, a1), a2)
        w0, w1, w2 = jnp.exp(a0 - mx), jnp.exp(a1 - mx), jnp.exp(a2 - mx)
        num = w0 * o0_ref[p] + w1 * o1_ref[p] + w2 * o2_ref[p]
        halves.append((num / (w0 + w1 + w2)).astype(BF16))
    attn = jnp.concatenate(halves, axis=1)
    a_br = jnp.dot(attn, wa_ref[...], preferred_element_type=F32)
    p_br = jnp.dot(pool_ref[...], wp_ref[...], preferred_element_type=F32)
    merged = gates[:, :d] * a_br + gates[:, d:] * p_br
    mix = jnp.dot(merged.astype(BF16), wo_ref[...], preferred_element_type=F32)
    xm = x + g1 * mix
    xmid_ref[...] = xm

    h2 = _norm_mod(xm, g2n_ref[...], sc2, sh2)
    h2b = h2.astype(BF16)
    h2l = (h2 - h2b.astype(F32)).astype(BF16)
    nt = (((1,), (1,)), ((), ()))
    rwh = rwh_ref[...]
    logits = (lax.dot_general(rwh, h2b, nt, preferred_element_type=F32)
              + lax.dot_general(rwl_ref[...], h2b, nt, preferred_element_type=F32)
              + lax.dot_general(rwh, h2l, nt, preferred_element_type=F32)
              + rb_ref[...])
    tm = logits.shape[1]

    e_io = lax.broadcasted_iota(I32, (N_EXPERTS, tm), 0)
    work = logits
    sels, vals = [], []
    for _ in range(TOP_K):
        mx = jnp.max(work, axis=0, keepdims=True)
        idx = jnp.min(jnp.where(work == mx, e_io, N_EXPERTS), axis=0, keepdims=True)
        sel = e_io == idx
        sels.append(sel)
        vals.append(mx)
        work = jnp.where(sel, -jnp.inf, work)
    exps = [jnp.exp(v - vals[0]) for v in vals]
    den = exps[0] + exps[1] + exps[2] + exps[3]
    for k in range(TOP_K):
        gate_ref[k:k + 1, :] = exps[k] / den

    triu = triu_ref[...]
    ohs = [jnp.where(s, 1.0, 0.0) for s in sels]
    withins = [jnp.dot(o.astype(BF16), triu, preferred_element_type=F32) for o in ohs]
    cnts = [jnp.sum(o, axis=1, keepdims=True) for o in ohs]
    n_e = cnts[0] + cnts[1] + cnts[2] + cnts[3]
    chunks = jnp.floor((n_e + (CHUNK - 1)) * (1.0 / CHUNK))
    chunks_l = jnp.broadcast_to(chunks, (N_EXPERTS, LANES))
    ltri = jnp.where(lax.broadcasted_iota(I32, (N_EXPERTS, N_EXPERTS), 0)
                     > lax.broadcasted_iota(I32, (N_EXPERTS, N_EXPERTS), 1), 1.0, 0.0)
    rstart_l = jnp.dot(ltri.astype(BF16), chunks_l.astype(BF16), preferred_element_type=F32)
    cmeta_ref[0] = chunks_l.astype(I32)
    cmeta_ref[1] = rstart_l.astype(I32)
    base = rstart_l[:, 0:1] * CHUNK
    lrows = []
    for k in range(TOP_K):
        lr = jnp.sum(ohs[k] * (base + withins[k]), axis=0, keepdims=True).astype(I32)
        lrow_ref[k:k + 1, :] = lr
        lrows.append(lr)
        base = base + cnts[k]

    blk = 256
    for r0 in range(0, R_T, blk):
        r_io = lax.broadcasted_iota(I32, (blk, tm), 0) + r0
        hit = jnp.where(r_io == lrows[0], 1.0,
              jnp.where(r_io == lrows[1], 1.0,
              jnp.where(r_io == lrows[2], 1.0,
              jnp.where(r_io == lrows[3], 1.0, 0.0))))
        rows = jnp.dot(hit.astype(BF16), h2b, preferred_element_type=F32)
        xs_ref[r0:r0 + blk, :] = _pack_bf16_pairs(rows)


def _epilogue(xt, mod_l, g1n, wg, attn_outs, pool, wa, wp, wo, g2n, rwh, rwl, rb, triu,
              batch, seq):
    t, d = xt.shape
    n_tiles = t // TM
    tpb = seq // TM
    row = lambda i: (i, 0)
    const = lambda i: (0, 0)
    half_spec = pl.BlockSpec((None, 2, TM, LANES), lambda i: (i // tpb, 0, i % tpb, 0))
    full = lambda a: pl.BlockSpec(a.shape, const)
    return pl.pallas_call(
        _epilogue_kernel,
        grid=(n_tiles,),
        in_specs=[
            pl.BlockSpec((TM, d), row),
            pl.BlockSpec((None, 6, d), lambda i: (i // tpb, 0, 0)),
            full(g1n), full(wg),
            half_spec, half_spec, half_spec, half_spec, half_spec, half_spec,
            pl.BlockSpec((TM, POOL_WIDTH), row),
            full(wa), full(wp), full(wo), full(g2n),
            full(rwh), full(rwl), full(rb), full(triu),
        ],
        out_specs=[
            pl.BlockSpec((TM, d), row),
            pl.BlockSpec((R_T, d // 2), row),
            pl.BlockSpec((TOP_K, TM), lambda i: (0, i)),
            pl.BlockSpec((TOP_K, TM), lambda i: (0, i)),
            pl.BlockSpec((None, 2, N_EXPERTS, LANES), lambda i: (i, 0, 0, 0)),
        ],
        out_shape=[
            jax.ShapeDtypeStruct((t, d), F32),
            jax.ShapeDtypeStruct((n_tiles * R_T, d // 2), U32),
            jax.ShapeDtypeStruct((TOP_K, t), I32),
            jax.ShapeDtypeStruct((TOP_K, t), F32),
            jax.ShapeDtypeStruct((n_tiles, 2, N_EXPERTS, LANES), I32),
        ],
        compiler_params=_cparams(1),
        name="epilogue",
    )(xt, mod_l, g1n, wg, *attn_outs, pool, wa, wp, wo, g2n, rwh, rwl, rb, triu)


def _route_kernel(cnt_ref, rs_ref, src_ref, bexp_ref, nact_ref, csrc_ref, *, n_tiles, n_blocks):
    def per_expert(e, carry):
        seg, last_e = carry

        def per_tile(j, tot):
            n = cnt_ref[j, e]
            d0 = seg * CPB + tot
            s0 = j * CPT + rs_ref[j, e]

            def per_chunk(i, c):
                src_ref[d0 + i] = s0 + i
                csrc_ref[s0 + i] = d0 + i
                return c
            lax.fori_loop(0, n, per_chunk, 0)
            return tot + n
        tot = lax.fori_loop(0, n_tiles, per_tile, 0)
        nb = lax.shift_right_logical(tot + (CPB - 1), int(np.log2(CPB)))

        def pad_slot(s, c):
            src_ref[s] = ZERO_CHUNK
            return c
        lax.fori_loop(seg * CPB + tot, (seg + nb) * CPB, pad_slot, 0)

        def set_block(b, c):
            bexp_ref[seg + b] = e
            return c
        lax.fori_loop(0, nb, set_block, 0)
        return seg + nb, jnp.where(nb > 0, e, last_e)

    nact, last_e = lax.fori_loop(0, N_EXPERTS, per_expert, (0, 0))
    nact_ref[0] = nact

    def idle_block(b, c):
        bexp_ref[b] = last_e

        def pad_slot(s, c2):
            src_ref[b * CPB + s] = ZERO_CHUNK
            return c2
        lax.fori_loop(0, CPB, pad_slot, 0)
        return c
    lax.fori_loop(nact, n_blocks, idle_block, 0)

    def per_tile_tail(j, c):
        used = lax.fori_loop(0, N_EXPERTS, lambda e, u: u + cnt_ref[j, e], 0)

        def pad_slot(s, c2):
            csrc_ref[j * CPT + s] = 0
            return c2
        lax.fori_loop(used, CPT, pad_slot, 0)
        return c
    lax.fori_loop(0, n_tiles, per_tile_tail, 0)


def _route(cnt, rs, n_blocks):
    n_tiles = cnt.shape[0]
    smem = pl.BlockSpec(memory_space=pltpu.SMEM)
    return pl.pallas_call(
        functools.partial(_route_kernel, n_tiles=n_tiles, n_blocks=n_blocks),
        in_specs=[smem, smem],
        out_specs=[smem, smem, smem, smem],
        out_shape=[
            jax.ShapeDtypeStruct((n_blocks * CPB,), I32),
            jax.ShapeDtypeStruct((n_blocks,), I32),
            jax.ShapeDtypeStruct((1,), I32),
            jax.ShapeDtypeStruct((n_tiles * CPT,), I32),
        ],
        name="route",
    )(cnt, rs)


def _chunk_ring(table_ref, src_hbm, buf, sem, step, n_steps, n_chunks):
    slot = lax.rem(step, 2)

    def start(which, to_slot):
        def issue(g, c):
            for u in range(DMA_UNROLL):
                i = g * DMA_UNROLL + u
                pltpu.make_async_copy(src_hbm.at[table_ref[which * n_chunks + i]],
                                      buf.at[to_slot, i], sem.at[to_slot]).start(priority=u % 2)
            return c
        lax.fori_loop(0, n_chunks // DMA_UNROLL, issue, 0)

    @pl.when(step == 0)
    def _():
        start(0, 0)

    @pl.when(step + 1 < n_steps)
    def _():
        start(step + 1, 1 - slot)

    pltpu.make_async_copy(src_hbm.at[pl.ds(0, n_chunks)], buf.at[slot], sem.at[slot]).wait()
    return slot


def _expert_kernel(bexp_ref, nact_ref, src_ref, xs_hbm, wgu_ref, bgu_ref, wd_ref, bd_ref,
                   ys_ref, xbuf, sem, wgu_b, wd_b, *, n_blocks):
    b = pl.program_id(0)
    slot = _chunk_ring(src_ref, xs_hbm, xbuf, sem, b, n_blocks, CPB)
    f = wd_ref.shape[0]

    @pl.when((b == 0) | (bexp_ref[b] != bexp_ref[jnp.maximum(b - 1, 0)]))
    def _():
        wgu_b[...] = wgu_ref[...].astype(BF16)
        wd_b[...] = wd_ref[...].astype(BF16)

    @pl.when(b < nact_ref[0])
    def _():
        x = _unpack_bf16_pairs(xbuf[slot].reshape(RB, xbuf.shape[-1]))
        z = jnp.dot(x, wgu_b[...], preferred_element_type=F32) + bgu_ref[...]
        glu = jnp.minimum(z[:, :f], SWIGLU_LIMIT)
        lin = jnp.clip(z[:, f:], -SWIGLU_LIMIT, SWIGLU_LIMIT)
        act = glu * jax.nn.sigmoid(SWIGLU_ALPHA * glu) * (lin + 1.0)
        y = jnp.dot(act.astype(BF16), wd_b[...], preferred_element_type=F32) + bd_ref[...]
        ys_ref[...] = _pack_bf16_pairs(y.astype(BF16).astype(F32))

    @pl.when(b >= nact_ref[0])
    def _():
        ys_ref[...] = jnp.zeros_like(ys_ref)


def _experts(xs, src, bexp, nact, wgu, bgu, wd, bd, layer, n_blocks):
    half = xs.shape[1]
    _, n_exp, d, f2 = wgu.shape
    f = wd.shape[2]
    xs3 = xs.reshape(xs.shape[0] // CHUNK, CHUNK, half)
    expert = lambda b, be, na, sr: (layer, be[b], 0, 0)
    grid_spec = pltpu.PrefetchScalarGridSpec(
        num_scalar_prefetch=3,
        grid=(n_blocks,),
        in_specs=[
            pl.BlockSpec(memory_space=pl.ANY),
            pl.BlockSpec((None, None, d, f2), expert),
            pl.BlockSpec((None, None, 1, f2), expert),
            pl.BlockSpec((None, None, f, d), expert),
            pl.BlockSpec((None, None, 1, d), expert),
        ],
        out_specs=pl.BlockSpec((RB, half), lambda b, be, na, sr: (b, 0)),
        scratch_shapes=[
            pltpu.VMEM((2, CPB, CHUNK, half), U32),
            pltpu.SemaphoreType.DMA((2,)),
            pltpu.VMEM((d, f2), BF16),
            pltpu.VMEM((f, d), BF16),
        ],
    )
    depth = wgu.shape[0]
    return pl.pallas_call(
        functools.partial(_expert_kernel, n_blocks=n_blocks),
        grid_spec=grid_spec,
        out_shape=jax.ShapeDtypeStruct((n_blocks * RB, half), U32),
        compiler_params=_cparams(1),
        name="experts",
    )(bexp, nact, src, xs3, wgu, bgu.reshape(depth, n_exp, 1, f2), wd,
      bd.reshape(depth, n_exp, 1, d))


def _combine_kernel(csrc_ref, xmid_ref, mod_ref, lrow_ref, gate_ref, fg_ref, ys_hbm,
                    out_ref, ybuf, sem, *, n_tiles, final):
    j = pl.program_id(0)
    slot = _chunk_ring(csrc_ref, ys_hbm, ybuf, sem, j, n_tiles, CPT)
    tm = xmid_ref.shape[0]
    y = _unpack_bf16_pairs(ybuf[slot].reshape(R_T, ybuf.shape[-1]))
    r_io = lax.broadcasted_iota(I32, (R_T, tm), 0)
    gt = jnp.zeros((R_T, tm), F32)
    for k in reversed(range(TOP_K)):
        gt = jnp.where(r_io == lrow_ref[k:k + 1, :], gate_ref[k:k + 1, :], gt)
    tn = (((0,), (0,)), ((), ()))
    moe = lax.dot_general(gt.astype(BF16), y, tn, preferred_element_type=F32)
    xn = xmid_ref[...] + mod_ref[5:6, :] * moe
    if final:
        ms = jnp.mean(xn * xn, axis=-1, keepdims=True)
        xn = xn * lax.rsqrt(ms + NORM_EPS) * fg_ref[...]
    out_ref[...] = xn


def _combine(csrc, xmid, mod_l, lrow, gate, fg, ys, seq, final):
    t, d = xmid.shape
    n_tiles = t // TM
    tpb = seq // TM
    half = ys.shape[1]
    ys3 = ys.reshape(ys.shape[0] // CHUNK, CHUNK, half)
    grid_spec = pltpu.PrefetchScalarGridSpec(
        num_scalar_prefetch=1,
        grid=(n_tiles,),
        in_specs=[
            pl.BlockSpec((TM, d), lambda i, cs: (i, 0)),
            pl.BlockSpec((None, 6, d), lambda i, cs: (i // tpb, 0, 0)),
            pl.BlockSpec((TOP_K, TM), lambda i, cs: (0, i)),
            pl.BlockSpec((TOP_K, TM), lambda i, cs: (0, i)),
            pl.BlockSpec((1, d), lambda i, cs: (0, 0)),
            pl.BlockSpec(memory_space=pl.ANY),
        ],
        out_specs=pl.BlockSpec((TM, d), lambda i, cs: (i, 0)),
        scratch_shapes=[
            pltpu.VMEM((2, CPT, CHUNK, half), U32),
            pltpu.SemaphoreType.DMA((2,)),
        ],
    )
    return pl.pallas_call(
        functools.partial(_combine_kernel, n_tiles=n_tiles, final=final),
        grid_spec=grid_spec,
        out_shape=jax.ShapeDtypeStruct((t, d), F32),
        compiler_params=_cparams(1),
        name="combine",
    )(csrc, xmid, mod_l, lrow, gate, fg, ys3)


def _rope_tables(seq):
    inv = ROPE_THETA ** (-jnp.arange(0, ROPE_DIM, 2, dtype=F32) / ROPE_DIM)
    ang = jnp.arange(seq, dtype=F32)[:, None] * inv[None, :]
    cos, sin = jnp.cos(ang), jnp.sin(ang)
    one = jnp.ones((seq, HEAD_DIM - ROPE_DIM), F32)
    zero8 = jnp.zeros((seq, ROPE_HALF), F32)
    zero48 = jnp.zeros((seq, HEAD_DIM - ROPE_DIM), F32)
    rc = jnp.concatenate([cos, cos, one], axis=1)
    rsa = jnp.concatenate([-sin, zero8, zero48], axis=1)
    rsb = jnp.concatenate([zero8, sin, zero48], axis=1)
    return tuple(jnp.tile(a, (1, LANES // HEAD_DIM)) for a in (rc, rsa, rsb))


def kernel(x, c, ada_w, ada_b, norm1_g, w_in, pool_w, pool_scale, w_attn_branch,
           w_pool_branch, w_out, norm2_g, router_w, router_b, w_gu, b_gu, w_down, b_down,
           final_g):
    batch, seq, d = x.shape
    depth = ada_w.shape[0]
    t = batch * seq
    assert seq % TM == 0 and t % TM == 0
    n_tiles = t // TM
    n_blocks = -(-(n_tiles * MAX_USED_CHUNKS * CHUNK + N_EXPERTS * (RB - CHUNK)) // RB)

    mod = _ada(c, ada_w, ada_b).reshape(depth, batch, 6, d)
    rope = _rope_tables(seq)
    triu = (np.arange(TM)[:, None] < np.arange(TM)[None, :]).astype(np.float32)
    triu = jnp.asarray(triu, BF16)
    gate_col0 = 3 * ATTN_WIDTH + POOL_WIDTH
    fg = final_g.reshape(1, d)

    xt = x.reshape(t, d)
    for l in range(depth):
        w_qkvu = w_in[l][:, :gate_col0].astype(BF16)
        w_gate = w_in[l][:, gate_col0:].astype(BF16)
        rwt = router_w[l].T
        rwh = rwt.astype(BF16)
        rwl = (rwt - rwh.astype(F32)).astype(BF16)
        g1n = norm1_g[l].reshape(1, d)
        g2n = norm2_g[l].reshape(1, d)

        qkv, u = _inproj(xt, mod[l], g1n, w_qkvu, rope, batch, seq)
        attn_outs = []
        for group in range(N_GROUPS):
            attn_outs.extend(_attention(qkv, group))
        pool = _pool(u, pool_w[l].astype(BF16), pool_scale[l].reshape(1, POOL_WIDTH),
                     batch, seq)
        xmid, xs, lrow, gate, cmeta = _epilogue(
            xt, mod[l], g1n, w_gate, attn_outs, pool,
            w_attn_branch[l].astype(BF16), w_pool_branch[l].astype(BF16),
            w_out[l].astype(BF16), g2n, rwh, rwl, router_b[l].reshape(N_EXPERTS, 1), triu,
            batch, seq)
        src, bexp, nact, csrc = _route(cmeta[:, 0, :, 0], cmeta[:, 1, :, 0], n_blocks)
        ys = _experts(xs, src, bexp, nact, w_gu, b_gu, w_down, b_down, l, n_blocks)
        xt = _combine(csrc, xmid, mod[l], lrow, gate, fg, ys, seq, final=(l == depth - 1))
    return xt.reshape(batch, seq, d)
```
